```python
import math
import jax, jax.numpy as jnp
from jax import lax
import numpy as np

D_MODEL = 4096
BATCH = 4
SEQ = 2048
DEPTH = 2

N_MIXERS = 2
N_A_LAYERS = (DEPTH + 1) // 2
N_B_LAYERS = DEPTH // 2
HEAD_DIM = 128
ATT_WIDTH = D_MODEL
N_HEADS = ATT_WIDTH // HEAD_DIM
Q_BLOCK = 128
SSM_WIDTH = D_MODEL
GROUP = 16
N_GROUPS = SSM_WIDTH // GROUP
STATE = 64
SCAN_CHUNK = 128
RMS_EPS = 1e-6
DT_MIN = 1e-3
DT_MAX = 1e-1

kernel_name = "hybrid_stickbreak_s5_decoder"


def _rmsnorm(x, g):
    xf = x.astype(jnp.float32)
    r = lax.rsqrt(jnp.mean(xf * xf, axis=-1, keepdims=True) + RMS_EPS)
    return (xf * r).astype(x.dtype) * g


def _stick_breaking_attention(q, k, v):
    S = q.shape[2]
    scale = 1.0 / math.sqrt(q.shape[-1])
    outs = []
    for blk in range(S // Q_BLOCK):
        q0 = blk * Q_BLOCK
        kl = q0 + Q_BLOCK
        qb = q[:, :, q0:kl]
        kb = k[:, :, :kl]
        vb = v[:, :, :kl]
        z = jnp.einsum('bhqd,bhkd->bhqk', qb, kb).astype(jnp.float32) * scale
        qpos = q0 + jnp.arange(Q_BLOCK)[:, None]
        kpos = jnp.arange(kl)[None, :]
        mask = kpos < qpos
        log_1m = jnp.where(mask, jax.nn.log_sigmoid(-z), 0.0)
        suffix = lax.cumsum(log_1m, axis=3, reverse=True) - log_1m
        w = jnp.where(mask, jnp.exp(jax.nn.log_sigmoid(z) + suffix), 0.0)
        outs.append(jnp.einsum('bhqk,bhkd->bhqd', w.astype(vb.dtype), vb))
    return jnp.concatenate(outs, axis=2)


def _attention_layer(x, norm_g, w_in, q_g, k_g, w_out):
    B, S, _ = x.shape
    h = _rmsnorm(x, norm_g)
    proj = h @ w_in
    q, k, v, gate = jnp.split(proj, 4, axis=-1)
    heads = lambda t: t.reshape(B, S, N_HEADS, HEAD_DIM)
    q = _rmsnorm(heads(q), q_g).transpose(0, 2, 1, 3)
    k = _rmsnorm(heads(k), k_g).transpose(0, 2, 1, 3)
    v = heads(v).transpose(0, 2, 1, 3)
    o = _stick_breaking_attention(q, k, v)
    o = o.transpose(0, 2, 1, 3).reshape(B, S, ATT_WIDTH).astype(x.dtype)
    return x + (o * jax.nn.silu(gate)) @ w_out


def _complex_affine_combine(e1, e2):
    a1r, a1i, b1r, b1i = e1
    a2r, a2i, b2r, b2i = e2
    ar = a2r * a1r - a2i * a1i
    ai = a2r * a1i + a2i * a1r
    br = a2r * b1r - a2i * b1i + b2r
    bi = a2r * b1i + a2i * b1r + b2i
    return ar, ai, br, bi


def _s5_scan(u, A_re, A_im, log_dt, B_re, B_im, C_re, C_im, D):
    Bsz, S, _ = u.shape
    f32 = jnp.float32
    A_re, A_im = A_re.astype(f32), A_im.astype(f32)
    B_re, B_im = B_re.astype(f32), B_im.astype(f32)
    C_re, C_im = C_re.astype(f32), C_im.astype(f32)
    dt = jnp.exp(log_dt.astype(f32))[:, None]
    mag = jnp.exp(A_re * dt)
    Ab_re = mag * jnp.cos(A_im * dt)
    Ab_im = mag * jnp.sin(A_im * dt)
    den = A_re * A_re + A_im * A_im
    f_re = ((Ab_re - 1.0) * A_re + Ab_im * A_im) / den
    f_im = (Ab_im * A_re - (Ab_re - 1.0) * A_im) / den
    Bb_re = f_re[..., None] * B_re - f_im[..., None] * B_im
    Bb_im = f_re[..., None] * B_im + f_im[..., None] * B_re
    Dg = D.astype(f32).reshape(N_GROUPS, GROUP)

    nc = S // SCAN_CHUNK
    uc = u.astype(f32).reshape(Bsz, nc, SCAN_CHUNK, N_GROUPS, GROUP).transpose(1, 2, 0, 3, 4)
    a_re_b = jnp.broadcast_to(Ab_re, (SCAN_CHUNK, Bsz, N_GROUPS, STATE))
    a_im_b = jnp.broadcast_to(Ab_im, (SCAN_CHUNK, Bsz, N_GROUPS, STATE))

    def chunk_step(carry, u_c):
        hp_re, hp_im = carry
        bu_re = jnp.einsum('cbgi,gpi->cbgp', u_c, Bb_re)
        bu_im = jnp.einsum('cbgi,gpi->cbgp', u_c, Bb_im)
        ac_re, ac_im, h_re, h_im = lax.associative_scan(
            _complex_affine_combine, (a_re_b, a_im_b, bu_re, bu_im), axis=0)
        h_re, h_im = (h_re + ac_re * hp_re[None] - ac_im * hp_im[None],
                      h_im + ac_re * hp_im[None] + ac_im * hp_re[None])
        y = (jnp.einsum('cbgp,gip->cbgi', h_re, C_re)
             - jnp.einsum('cbgp,gip->cbgi', h_im, C_im)
             + Dg * u_c)
        return (h_re[-1], h_im[-1]), y

    h0 = jnp.zeros((Bsz, N_GROUPS, STATE), f32)
    _, ys = lax.scan(chunk_step, (h0, h0), uc)
    return ys.transpose(2, 0, 1, 3, 4).reshape(Bsz, S, SSM_WIDTH)


def _ssm_layer(x, norm_g, w_in, A_re, A_im, log_dt, B_re, B_im, C_re, C_im, D, glu_w, glu_b, w_out):
    h = _rmsnorm(x, norm_g)
    proj = h @ w_in
    u, gate = jnp.split(proj, 2, axis=-1)
    y = _s5_scan(u, A_re, A_im, log_dt, B_re, B_im, C_re, C_im, D).astype(x.dtype)
    y = jax.nn.gelu(y)
    y = y * jax.nn.sigmoid(y @ glu_w + glu_b)
    return x + (y * jax.nn.silu(gate)) @ w_out


def setup_inputs(seed: int = 0) -> dict:
    key = jax.random.key(seed)
    ks = jax.random.split(key, 20)
    n = jax.random.normal
    f32 = jnp.float32
    x = n(ks[0], (BATCH, SEQ, D_MODEL), f32)
    norm_g = 1.0 + 0.02 * n(ks[1], (DEPTH, D_MODEL), f32)
    attn_w_in = n(ks[2], (N_A_LAYERS, D_MODEL, 4 * ATT_WIDTH), f32) * D_MODEL ** -0.5
    attn_q_g = 1.0 + 0.02 * n(ks[3], (N_A_LAYERS, HEAD_DIM), f32)
    attn_k_g = 1.0 + 0.02 * n(ks[4], (N_A_LAYERS, HEAD_DIM), f32)
    attn_w_out = n(ks[5], (N_A_LAYERS, ATT_WIDTH, D_MODEL), f32) * ATT_WIDTH ** -0.5
    ssm_w_in = n(ks[6], (N_B_LAYERS, D_MODEL, 2 * SSM_WIDTH), f32) * D_MODEL ** -0.5
    ssm_A_re = -0.5 + 0.01 * n(ks[7], (N_B_LAYERS, N_GROUPS, STATE), f32)
    ssm_A_im = (math.pi * jnp.arange(STATE, dtype=f32)
                + 0.01 * n(ks[8], (N_B_LAYERS, N_GROUPS, STATE), f32))
    ssm_log_dt = jax.random.uniform(ks[9], (N_B_LAYERS, N_GROUPS), f32,
                                    math.log(DT_MIN), math.log(DT_MAX))
    ssm_B_re = n(ks[10], (N_B_LAYERS, N_GROUPS, STATE, GROUP), f32) * (2 * GROUP) ** -0.5
    ssm_B_im = n(ks[11], (N_B_LAYERS, N_GROUPS, STATE, GROUP), f32) * (2 * GROUP) ** -0.5
    ssm_C_re = n(ks[12], (N_B_LAYERS, N_GROUPS, GROUP, STATE), f32) * STATE ** -0.5
    ssm_C_im = n(ks[13], (N_B_LAYERS, N_GROUPS, GROUP, STATE), f32) * STATE ** -0.5
    ssm_D = n(ks[14], (N_B_LAYERS, SSM_WIDTH), f32)
    ssm_glu_w = n(ks[15], (N_B_LAYERS, SSM_WIDTH, SSM_WIDTH), f32) * SSM_WIDTH ** -0.5
    ssm_glu_b = 0.01 * n(ks[16], (N_B_LAYERS, SSM_WIDTH), f32)
    ssm_w_out = n(ks[17], (N_B_LAYERS, SSM_WIDTH, D_MODEL), f32) * SSM_WIDTH ** -0.5
    return {"x": x, "norm_g": norm_g, "attn_w_in": attn_w_in, "attn_q_g": attn_q_g,
            "attn_k_g": attn_k_g, "attn_w_out": attn_w_out, "ssm_w_in": ssm_w_in,
            "ssm_A_re": ssm_A_re, "ssm_A_im": ssm_A_im, "ssm_log_dt": ssm_log_dt,
            "ssm_B_re": ssm_B_re, "ssm_B_im": ssm_B_im, "ssm_C_re": ssm_C_re,
            "ssm_C_im": ssm_C_im, "ssm_D": ssm_D, "ssm_glu_w": ssm_glu_w,
            "ssm_glu_b": ssm_glu_b, "ssm_w_out": ssm_w_out}


def reference(x, norm_g, attn_w_in, attn_q_g, attn_k_g, attn_w_out, ssm_w_in, ssm_A_re, ssm_A_im,
              ssm_log_dt, ssm_B_re, ssm_B_im, ssm_C_re, ssm_C_im, ssm_D, ssm_glu_w, ssm_glu_b,
              ssm_w_out):
    for i in range(DEPTH):
        j = i // N_MIXERS
        if i % N_MIXERS == 0:
            x = _attention_layer(x, norm_g[i], attn_w_in[j], attn_q_g[j], attn_k_g[j], attn_w_out[j])
        else:
            x = _ssm_layer(x, norm_g[i], ssm_w_in[j], ssm_A_re[j], ssm_A_im[j], ssm_log_dt[j],
                           ssm_B_re[j], ssm_B_im[j], ssm_C_re[j], ssm_C_im[j], ssm_D[j],
                           ssm_glu_w[j], ssm_glu_b[j], ssm_w_out[j])
    return x
```

```python
import functools
import math

import jax
import jax.numpy as jnp
from jax import lax
from jax.experimental import pallas as pl
from jax.experimental.pallas import tpu as pltpu

HEAD_DIM = 128
GROUP = 16
STATE = 64
RMS_EPS = 1e-6
LANES = 128
CHUNK = 16
GROUPS_PER_TILE = LANES // GROUP
VMEM_CAP_BYTES = 60 * 1024 * 1024

F32 = jnp.float32
BF16 = jnp.bfloat16


def _vmem_limit(estimate_bytes):
    return int(min(VMEM_CAP_BYTES, max(32 * 1024 * 1024, estimate_bytes)))


def _rmsnorm_kernel(x_ref, g_ref, o_ref):
    x = x_ref[...]
    r = lax.rsqrt(jnp.mean(x * x, axis=-1, keepdims=True) + RMS_EPS)
    o_ref[...] = ((x * r) * g_ref[...]).astype(o_ref.dtype)


def _rmsnorm(x2, g):
    m, d = x2.shape
    tm = min(256, m)
    return pl.pallas_call(
        _rmsnorm_kernel,
        out_shape=jax.ShapeDtypeStruct((m, d), BF16),
        grid=(m // tm,),
        in_specs=[pl.BlockSpec((tm, d), lambda i: (i, 0)),
                  pl.BlockSpec((1, d), lambda i: (0, 0))],
        out_specs=pl.BlockSpec((tm, d), lambda i: (i, 0)),
        compiler_params=pltpu.CompilerParams(dimension_semantics=("parallel",)),
        name="rmsnorm",
    )(x2, g.reshape(1, d))


def _mm_kernel(x_ref, w_ref, *rest, mode, scale):
    acc = jnp.dot(x_ref[...], w_ref[...], preferred_element_type=F32)
    o_ref = rest[-1]
    if mode == "headnorm":
        g = rest[0][...]
        for h in range(acc.shape[1] // HEAD_DIM):
            sl = slice(h * HEAD_DIM, (h + 1) * HEAD_DIM)
            a = acc[:, sl]
            r = lax.rsqrt(jnp.mean(a * a, axis=-1, keepdims=True) + RMS_EPS)
            o_ref[:, sl] = (((a * r) * g) * scale).astype(o_ref.dtype)
    elif mode == "silu":
        o_ref[...] = (acc * jax.nn.sigmoid(acc)).astype(o_ref.dtype)
    elif mode == "residual":
        o_ref[...] = (rest[0][...] + acc).astype(o_ref.dtype)
    elif mode == "glu":
        b_ref, y_ref, gate_ref = rest[0], rest[1], rest[2]
        t = acc + b_ref[...]
        o_ref[...] = (y_ref[...].astype(F32) * jax.nn.sigmoid(t) * gate_ref[...].astype(F32)).astype(o_ref.dtype)
    else:
        o_ref[...] = acc.astype(o_ref.dtype)


def _matmul(x, w, *, col_off, n, mode, out_dtype, extras=(), scale=1.0, tm=1024, tn=1024):
    m, k = x.shape
    tm = min(tm, m)
    tn = min(tn, n)
    off = col_off // tn
    in_specs = [pl.BlockSpec((tm, k), lambda i, j: (i, 0)),
                pl.BlockSpec((k, tn), lambda i, j: (0, j + off))]
    extra_bytes = 0
    for e in extras:
        if e.shape[0] == 1:
            width = e.shape[1]
            if width == n:
                in_specs.append(pl.BlockSpec((1, tn), lambda i, j: (0, j)))
            else:
                in_specs.append(pl.BlockSpec((1, width), lambda i, j: (0, 0)))
        else:
            in_specs.append(pl.BlockSpec((tm, tn), lambda i, j: (i, j)))
            extra_bytes += tm * tn * e.dtype.itemsize
    out_bytes = tm * tn * jnp.dtype(out_dtype).itemsize
    est = 2 * (tm * k * 2 + k * tn * 2 + out_bytes + extra_bytes) + 3 * tm * tn * 4
    return pl.pallas_call(
        functools.partial(_mm_kernel, mode=mode, scale=scale),
        out_shape=jax.ShapeDtypeStruct((m, n), out_dtype),
        grid=(m // tm, n // tn),
        in_specs=in_specs,
        out_specs=pl.BlockSpec((tm, tn), lambda i, j: (i, j)),
        compiler_params=pltpu.CompilerParams(
            dimension_semantics=("parallel", "parallel"),
            vmem_limit_bytes=_vmem_limit(est + (4 << 20))),
        name="mm_" + mode,
    )(x, w, *extras)


def _attn_kernel(q_ref, k_ref, v_ref, g_ref, o_ref, *, tile):
    s_len = q_ref.shape[0]
    nq = s_len // tile
    row = lax.broadcasted_iota(jnp.int32, (tile, tile), 0)
    col = lax.broadcasted_iota(jnp.int32, (tile, tile), 1)
    causal = col < row
    later = (row > col).astype(BF16)

    def tile_step(q, j, carry, o, diag):
        k = k_ref[pl.ds(pl.multiple_of(j * tile, tile), tile), :]
        v = v_ref[pl.ds(pl.multiple_of(j * tile, tile), tile), :]
        z = lax.dot_general(q, k, (((1,), (1,)), ((), ())), preferred_element_type=F32)
        sp = jnp.maximum(z, 0.0) + jnp.log1p(jnp.exp(-jnp.abs(z)))
        l1m = -sp
        if diag:
            l1m = jnp.where(causal, l1m, 0.0)
        suffix = jnp.dot(l1m.astype(BF16), later, preferred_element_type=F32)
        w = jnp.exp((z - sp) + suffix + carry)
        if diag:
            w = jnp.where(causal, w, 0.0)
        o = o + jnp.dot(w.astype(BF16), v, preferred_element_type=F32)
        carry = carry + jnp.sum(l1m, axis=1, keepdims=True)
        return carry, o

    def q_step(i, _):
        rows = pl.ds(pl.multiple_of(i * tile, tile), tile)
        q = q_ref[rows, :]
        carry = jnp.zeros((tile, 1), F32)
        o = jnp.zeros((tile, HEAD_DIM), F32)
        carry, o = tile_step(q, i, carry, o, True)

        def k_step(t, co):
            return tile_step(q, i - 1 - t, co[0], co[1], False)

        carry, o = lax.fori_loop(0, i, k_step, (carry, o))
        o_ref[rows, :] = (o * g_ref[rows, :].astype(F32)).astype(o_ref.dtype)
        return 0

    lax.fori_loop(0, nq, q_step, 0)


def _attention(q, k, v, gate, *, batch, seq):
    m, d = q.shape
    heads = d // HEAD_DIM
    tile = min(256, seq)
    r3 = lambda a: a.reshape(batch, seq, d)
    spec = pl.BlockSpec((None, seq, HEAD_DIM), lambda b, h: (b, 0, h))
    out = pl.pallas_call(
        functools.partial(_attn_kernel, tile=tile),
        out_shape=jax.ShapeDtypeStruct((batch, seq, d), BF16),
        grid=(batch, heads),
        in_specs=[spec, spec, spec, spec],
        out_specs=spec,
        compiler_params=pltpu.CompilerParams(dimension_semantics=("parallel", "parallel")),
        name="stickbreak_attn",
    )(r3(q), r3(k), r3(v), r3(gate))
    return out.reshape(m, d)


def _ssm_prep_kernel(p_ref, bre_ref, bim_ref, cre_ref, cim_ref, t_ref, w_ref, vt_ref, lam_ref):
    n_rows = 24
    n = lax.broadcasted_iota(jnp.int32, (n_rows, LANES), 0).astype(F32)
    lane = lax.broadcasted_iota(jnp.int32, (GROUP, LANES), 1)

    def per_group(gi, _):
        a_re = p_ref[gi, 0:1, :]
        a_im = p_ref[gi, 1:2, :]
        dt = jnp.exp(p_ref[gi, 2:3, :])
        half_mask = p_ref[gi, 3:4, :]
        mag = jnp.exp(n * (a_re * dt))
        ang = n * (a_im * dt)
        pre = mag * jnp.cos(ang)
        pim = mag * jnp.sin(ang)
        lam_re = pre[1:2, :]
        lam_im = pim[1:2, :]
        den = a_re * a_re + a_im * a_im
        f_re = ((lam_re - 1.0) * a_re + lam_im * a_im) / den
        f_im = (lam_im * a_re - (lam_re - 1.0) * a_im) / den
        fre = f_re * pre - f_im * pim
        fim = f_re * pim + f_im * pre
        bre = bre_ref[gi]
        bim = bim_ref[gi]
        cre = cre_ref[gi]
        cim = cim_ref[gi]

        v_blocks = []
        for e in range(CHUNK + 1):
            va = cre * pre[e:e + 1, :] - cim * pim[e:e + 1, :]
            vb = -(cre * pim[e:e + 1, :] + cim * pre[e:e + 1, :])
            v_blocks.append(jnp.concatenate([va, vb], axis=1))
        for t in range(CHUNK):
            vt_ref[gi, t * GROUP:(t + 1) * GROUP, :] = v_blocks[t + 1].astype(vt_ref.dtype)
        v0t = jnp.concatenate(v_blocks[:CHUNK], axis=0)

        for s in range(CHUNK):
            e = CHUNK - 1 - s
            wa = fre[e:e + 1, :] * bre - fim[e:e + 1, :] * bim
            wb = fre[e:e + 1, :] * bim + fim[e:e + 1, :] * bre
            w_ref[gi, s * GROUP:(s + 1) * GROUP, :] = jnp.concatenate([wa, wb], axis=1).astype(w_ref.dtype)

        wb0 = jnp.concatenate([fre[0:1, :] * bre - fim[0:1, :] * bim,
                               fre[0:1, :] * bim + fim[0:1, :] * bre], axis=1)
        kmat = lax.dot_general(wb0, v0t, (((1,), (1,)), ((), ())),
                               precision=lax.Precision.HIGHEST, preferred_element_type=F32)
        k0 = kmat[:, :LANES]
        k1 = kmat[:, LANES:]
        zero = jnp.zeros_like(k0)
        for s in range(CHUNK):
            sh = s * GROUP
            if s == 0:
                lo, hi = k0, k1
            elif sh < LANES:
                r0 = pltpu.roll(k0, sh, 1)
                r1 = pltpu.roll(k1, sh, 1)
                lo = jnp.where(lane >= sh, r0, 0.0)
                hi = jnp.where(lane >= sh, r1, r0)
            elif sh == LANES:
                lo, hi = zero, k0
            else:
                r0 = pltpu.roll(k0, sh - LANES, 1)
                lo = zero
                hi = jnp.where(lane >= sh - LANES, r0, 0.0)
            t_ref[gi, s * GROUP:(s + 1) * GROUP, :] = jnp.concatenate([lo, hi], axis=1).astype(t_ref.dtype)

        lam_rows = jnp.concatenate([pre[CHUNK:CHUNK + 1, :] * half_mask, pim[CHUNK:CHUNK + 1, :] * half_mask,
                                    jnp.zeros((6, LANES), F32)], axis=0)
        lam_ref[gi] = lam_rows
        return 0

    lax.fori_loop(0, p_ref.shape[0], per_group, 0)


def _ssm_prep(a_re, a_im, log_dt, b_re, b_im, c_re, c_im):
    g = a_re.shape[0]
    even = (jnp.arange(g) % 2 == 0)[:, None, None]

    def pad_half(a):
        z = jnp.zeros_like(a)
        return jnp.where(even, jnp.concatenate([a, z], -1), jnp.concatenate([z, a], -1))

    dup = lambda a: jnp.concatenate([a, a], -1)
    half_mask = jnp.where(even[:, 0], jnp.concatenate([jnp.ones((g, STATE)), jnp.zeros((g, STATE))], -1),
                          jnp.concatenate([jnp.zeros((g, STATE)), jnp.ones((g, STATE))], -1))
    params = jnp.stack([dup(a_re), dup(a_im), jnp.broadcast_to(log_dt[:, None], (g, LANES)), half_mask]
                       + [jnp.zeros((g, LANES), F32)] * 4, axis=1).astype(F32)
    bt_re = pad_half(jnp.swapaxes(b_re, 1, 2))
    bt_im = pad_half(jnp.swapaxes(b_im, 1, 2))
    cp_re = pad_half(c_re)
    cp_im = pad_half(c_im)
    gb = GROUPS_PER_TILE
    kk = CHUNK * GROUP
    spec8 = pl.BlockSpec((gb, 8, LANES), lambda i: (i, 0, 0))
    spec16 = pl.BlockSpec((gb, GROUP, LANES), lambda i: (i, 0, 0))
    specm = pl.BlockSpec((gb, kk, kk), lambda i: (i, 0, 0))
    return pl.pallas_call(
        _ssm_prep_kernel,
        out_shape=(jax.ShapeDtypeStruct((g, kk, kk), BF16),
                   jax.ShapeDtypeStruct((g, kk, 2 * LANES), BF16),
                   jax.ShapeDtypeStruct((g, kk, 2 * LANES), BF16),
                   jax.ShapeDtypeStruct((g, 8, LANES), F32)),
        grid=(g // gb,),
        in_specs=[spec8, spec16, spec16, spec16, spec16],
        out_specs=(specm, specm, specm, spec8),
        compiler_params=pltpu.CompilerParams(dimension_semantics=("parallel",)),
        name="ssm_prep",
    )(params, bt_re, bt_im, cp_re, cp_im)


def _ssm_kernel(u_ref, t_ref, w_ref, vt_ref, lam_ref, d_ref, o_ref, zs, ys, ss, hs, yscr, *, batch, nchunk):
    rows_per_step = 32
    n_rows = batch * nchunk
    pairs = GROUPS_PER_TILE // 2
    lane_group = lax.broadcasted_iota(jnp.int32, (rows_per_step, LANES), 1) // GROUP
    bit_set = {d: (lane_group & d) != 0 for d in (4, 2, 1)}

    def transpose_pieces(xs):
        for d in (4, 2, 1):
            new = list(xs)
            for m in range(GROUPS_PER_TILE):
                if m & d:
                    continue
                a, b = xs[m], xs[m + d]
                new[m] = jnp.where(bit_set[d], pltpu.roll(b, GROUP * d, 1), a)
                new[m + d] = jnp.where(bit_set[d], b, pltpu.roll(a, LANES - GROUP * d, 1))
            xs = new
        return xs

    def fold(rb, _):
        r0 = pl.multiple_of(rb * rows_per_step, rows_per_step)
        for half in range(2):
            xs = [u_ref[pl.ds(r0 * CHUNK + half * 8 + m, rows_per_step, stride=CHUNK), :]
                  for m in range(GROUPS_PER_TILE)]
            pieces = transpose_pieces(xs)
            for g in range(GROUPS_PER_TILE):
                zs[g, pl.ds(r0, rows_per_step), half * LANES:(half + 1) * LANES] = pieces[g].astype(zs.dtype)
        return 0

    lax.fori_loop(0, n_rows // rows_per_step, fold, 0)

    for p in range(pairs):
        s_end = (jnp.dot(zs[2 * p], w_ref[2 * p], preferred_element_type=F32)
                 + jnp.dot(zs[2 * p + 1], w_ref[2 * p + 1], preferred_element_type=F32))
        ss[2 * p] = s_end[:, :LANES]
        ss[2 * p + 1] = s_end[:, LANES:]

    lre = [lam_ref[2 * p, 0:1, :] + lam_ref[2 * p + 1, 0:1, :] for p in range(pairs)]
    lim = [lam_ref[2 * p, 1:2, :] + lam_ref[2 * p + 1, 1:2, :] for p in range(pairs)]

    def scan(c, carry):
        new = []
        for p in range(pairs):
            h_re, h_im = carry[2 * p], carry[2 * p + 1]
            rows = pl.ds(c, batch, stride=nchunk)
            hs[2 * p, rows, :] = h_re
            hs[2 * p + 1, rows, :] = h_im
            s_re = ss[2 * p, rows, :]
            s_im = ss[2 * p + 1, rows, :]
            new.append(h_re * lre[p] - h_im * lim[p] + s_re)
            new.append(h_im * lre[p] + h_re * lim[p] + s_im)
        return tuple(new)

    zero_state = jnp.zeros((batch, LANES), F32)
    lax.fori_loop(0, nchunk, scan, tuple(zero_state for _ in range(2 * pairs)))

    for g in range(GROUPS_PER_TILE):
        h_in = jnp.concatenate([hs[2 * (g // 2)], hs[2 * (g // 2) + 1]], axis=1).astype(BF16)
        ys[g] = (jnp.dot(zs[g], t_ref[g], preferred_element_type=F32)
                 + lax.dot_general(h_in, vt_ref[g], (((1,), (1,)), ((), ())), preferred_element_type=F32))

    d_row = d_ref[...]

    def unfold(rb, _):
        r0 = pl.multiple_of(rb * rows_per_step, rows_per_step)
        for half in range(2):
            xs = [ys[g, pl.ds(r0, rows_per_step), half * LANES:(half + 1) * LANES]
                  for g in range(GROUPS_PER_TILE)]
            pieces = transpose_pieces(xs)
            for m in range(GROUPS_PER_TILE):
                rows = pl.ds(r0 * CHUNK + half * 8 + m, rows_per_step, stride=CHUNK)
                yscr[rows, :] = pieces[m] + d_row * u_ref[rows, :]
        return 0

    lax.fori_loop(0, n_rows // rows_per_step, unfold, 0)

    dense_rows = 256

    def activate(i, _):
        rows = pl.ds(pl.multiple_of(i * dense_rows, dense_rows), dense_rows)
        o_ref[rows, :] = jax.nn.gelu(yscr[rows, :]).astype(o_ref.dtype)
        return 0

    lax.fori_loop(0, (n_rows * CHUNK) // dense_rows, activate, 0)


def _ssm(u, tmat, wmat, vtmat, lam, d_vec, *, batch, seq):
    m, d = u.shape
    nchunk = seq // CHUNK
    n_rows = batch * nchunk
    gb = GROUPS_PER_TILE
    kk = CHUNK * GROUP
    specm = pl.BlockSpec((gb, kk, kk), lambda i: (i, 0, 0))
    est = (2 * (m * LANES * 4 + m * LANES * 2 + 3 * gb * kk * kk * 2)
           + gb * n_rows * kk * (2 + 4) + 2 * (gb // 2) * n_rows * kk * 4 + m * LANES * 4)
    return pl.pallas_call(
        functools.partial(_ssm_kernel, batch=batch, nchunk=nchunk),
        out_shape=jax.ShapeDtypeStruct((m, d), BF16),
        grid=(d // LANES,),
        in_specs=[pl.BlockSpec((m, LANES), lambda i: (0, i)),
                  specm, specm, specm,
                  pl.BlockSpec((gb, 8, LANES), lambda i: (i, 0, 0)),
                  pl.BlockSpec((1, LANES), lambda i: (0, i))],
        out_specs=pl.BlockSpec((m, LANES), lambda i: (0, i)),
        scratch_shapes=[pltpu.VMEM((gb, n_rows, kk), BF16),
                        pltpu.VMEM((gb, n_rows, kk), F32),
                        pltpu.VMEM((gb, n_rows, LANES), F32),
                        pltpu.VMEM((gb, n_rows, LANES), F32),
                        pltpu.VMEM((m, LANES), F32)],
        compiler_params=pltpu.CompilerParams(
            dimension_semantics=("parallel",),
            vmem_limit_bytes=_vmem_limit(est + (4 << 20))),
        name="s5_scan",
    )(u, tmat, wmat, vtmat, lam, d_vec.reshape(1, d))


def kernel(x, norm_g, attn_w_in, attn_q_g, attn_k_g, attn_w_out, ssm_w_in, ssm_A_re, ssm_A_im, ssm_log_dt,
           ssm_B_re, ssm_B_im, ssm_C_re, ssm_C_im, ssm_D, ssm_glu_w, ssm_glu_b, ssm_w_out):
    batch, seq, d = x.shape
    m = batch * seq
    assert d % LANES == 0 and seq % 256 == 0 and m % 256 == 0
    assert ssm_A_re.shape[1:] == (d // GROUP, STATE) and (d // GROUP) % GROUPS_PER_TILE == 0
    x2 = x.reshape(m, d)

    h = _rmsnorm(x2, norm_g[0])
    w_in = attn_w_in[0].astype(BF16)
    qg = attn_q_g[0].reshape(1, HEAD_DIM)
    kg = attn_k_g[0].reshape(1, HEAD_DIM)
    q = _matmul(h, w_in, col_off=0, n=d, mode="headnorm", out_dtype=BF16, extras=(qg,),
                scale=1.0 / math.sqrt(HEAD_DIM))
    k = _matmul(h, w_in, col_off=d, n=d, mode="headnorm", out_dtype=BF16, extras=(kg,))
    v = _matmul(h, w_in, col_off=2 * d, n=d, mode="store", out_dtype=BF16)
    gate = _matmul(h, w_in, col_off=3 * d, n=d, mode="silu", out_dtype=BF16)
    og = _attention(q, k, v, gate, batch=batch, seq=seq)
    x2 = _matmul(og, attn_w_out[0].astype(BF16), col_off=0, n=d, mode="residual", out_dtype=F32,
                 extras=(x2,), tn=512)

    h = _rmsnorm(x2, norm_g[1])
    w_in = ssm_w_in[0].astype(BF16)
    u = _matmul(h, w_in, col_off=0, n=d, mode="store", out_dtype=F32, tn=512)
    gate = _matmul(h, w_in, col_off=d, n=d, mode="silu", out_dtype=BF16)
    tmat, wmat, vtmat, lam = _ssm_prep(ssm_A_re[0], ssm_A_im[0], ssm_log_dt[0], ssm_B_re[0], ssm_B_im[0],
                                       ssm_C_re[0], ssm_C_im[0])
    y = _ssm(u, tmat, wmat, vtmat, lam, ssm_D[0], batch=batch, seq=seq)
    yg = _matmul(y, ssm_glu_w[0].astype(BF16), col_off=0, n=d, mode="glu", out_dtype=BF16,
                 extras=(ssm_glu_b[0].reshape(1, d), y, gate), tn=512)
    x2 = _matmul(yg, ssm_w_out[0].astype(BF16), col_off=0, n=d, mode="residual", out_dtype=F32,
                 extras=(x2,), tn=512)
    return x2.reshape(batch, seq, d)
```

```python
import functools
import math

import jax
import jax.numpy as jnp
from jax import lax
from jax.experimental import pallas as pl
from jax.experimental.pallas import tpu as pltpu

HEAD_DIM = 128
GROUP = 16
STATE = 64
RMS_EPS = 1e-6
LANES = 128
CHUNK = 16
LOG2E = 1.4426950408889634
GROUPS_PER_TILE = LANES // GROUP
VMEM_CAP_BYTES = 60 * 1024 * 1024

F32 = jnp.float32
BF16 = jnp.bfloat16


def _vmem_limit(estimate_bytes):
    return int(min(VMEM_CAP_BYTES, max(32 * 1024 * 1024, estimate_bytes)))


def _rmsnorm_kernel(x_ref, g_ref, o_ref):
    x = x_ref[...]
    r = lax.rsqrt(jnp.mean(x * x, axis=-1, keepdims=True) + RMS_EPS)
    o_ref[...] = ((x * r) * g_ref[...]).astype(o_ref.dtype)


def _rmsnorm(x2, g):
    m, d = x2.shape
    tm = min(256, m)
    return pl.pallas_call(
        _rmsnorm_kernel,
        out_shape=jax.ShapeDtypeStruct((m, d), BF16),
        grid=(m // tm,),
        in_specs=[pl.BlockSpec((tm, d), lambda i: (i, 0)),
                  pl.BlockSpec((1, d), lambda i: (0, 0))],
        out_specs=pl.BlockSpec((tm, d), lambda i: (i, 0)),
        compiler_params=pltpu.CompilerParams(dimension_semantics=("parallel",)),
        name="rmsnorm",
    )(x2, g.reshape(1, d))


def _mm_kernel(x_ref, w_ref, *rest, mode, scale):
    acc = jnp.dot(x_ref[...], w_ref[...].astype(BF16), preferred_element_type=F32)
    o_ref = rest[-1]
    if mode == "headnorm":
        g = rest[0][...]
        for h in range(acc.shape[1] // HEAD_DIM):
            sl = slice(h * HEAD_DIM, (h + 1) * HEAD_DIM)
            a = acc[:, sl]
            r = lax.rsqrt(jnp.mean(a * a, axis=-1, keepdims=True) + RMS_EPS)
            o_ref[:, sl] = (((a * r) * g) * scale).astype(o_ref.dtype)
    elif mode == "silu":
        o_ref[...] = (acc * jax.nn.sigmoid(acc)).astype(o_ref.dtype)
    elif mode == "residual":
        o_ref[...] = (rest[0][...] + acc).astype(o_ref.dtype)
    elif mode == "glu":
        b_ref, y_ref, gate_ref = rest[0], rest[1], rest[2]
        t = acc + b_ref[...]
        o_ref[...] = (y_ref[...].astype(F32) * jax.nn.sigmoid(t) * gate_ref[...].astype(F32)).astype(o_ref.dtype)
    else:
        o_ref[...] = acc.astype(o_ref.dtype)


def _matmul(x, w, *, col_off, n, mode, out_dtype, extras=(), scale=1.0, tm=1024, tn=512):
    m, k = x.shape
    tm = min(tm, m)
    tn = min(tn, n)
    off = col_off // tn
    in_specs = [pl.BlockSpec((tm, k), lambda j, i: (i, 0)),
                pl.BlockSpec((k, tn), lambda j, i: (0, j + off))]
    extra_bytes = 0
    for e in extras:
        if e.shape[0] == 1:
            width = e.shape[1]
            if width == n:
                in_specs.append(pl.BlockSpec((1, tn), lambda j, i: (0, j)))
            else:
                in_specs.append(pl.BlockSpec((1, width), lambda j, i: (0, 0)))
        else:
            in_specs.append(pl.BlockSpec((tm, tn), lambda j, i: (i, j)))
            extra_bytes += tm * tn * e.dtype.itemsize
    out_bytes = tm * tn * jnp.dtype(out_dtype).itemsize
    est = (2 * (tm * k * 2 + k * tn * w.dtype.itemsize + out_bytes + extra_bytes)
           + k * tn * 2 + 3 * tm * tn * 4)
    return pl.pallas_call(
        functools.partial(_mm_kernel, mode=mode, scale=scale),
        out_shape=jax.ShapeDtypeStruct((m, n), out_dtype),
        grid=(n // tn, m // tm),
        in_specs=in_specs,
        out_specs=pl.BlockSpec((tm, tn), lambda j, i: (i, j)),
        compiler_params=pltpu.CompilerParams(
            dimension_semantics=("parallel", "parallel"),
            vmem_limit_bytes=_vmem_limit(est + (4 << 20))),
        name="mm_" + mode,
    )(x, w, *extras)


def _attn_kernel(q_ref, k_ref, v_ref, g_ref, o_ref, acc_ref, carry_ref, *, tile):
    s_len = q_ref.shape[0]
    nt = s_len // tile
    row = lax.broadcasted_iota(jnp.int32, (tile, tile), 0)
    col = lax.broadcasted_iota(jnp.int32, (tile, tile), 1)
    causal = col < row
    neg_later = jnp.where(row > col, -1.0, 0.0).astype(BF16)
    sign_bit = jnp.uint32(0x80000000)
    for j in reversed(range(nt)):
        first = j == nt - 1
        lo, hi = j * tile, (j + 1) * tile
        k = k_ref[lo:hi, :]
        v = v_ref[lo:hi, :]
        z = lax.dot_general(q_ref[lo:, :], k, (((1,), (1,)), ((), ())), preferred_element_type=F32)
        neg_abs = pltpu.bitcast(pltpu.bitcast(z, jnp.uint32) | sign_bit, F32)
        sp = jnp.maximum(z, 0.0) + jnp.log2(1.0 + jnp.exp2(neg_abs))
        sp_diag = jnp.where(causal, sp[:tile], 0.0)
        sp = sp_diag if first else jnp.concatenate([sp_diag, sp[tile:]], axis=0)
        suffix = jnp.dot(sp.astype(BF16), neg_later, preferred_element_type=F32)
        arg = (z - sp) + suffix
        if not first:
            carry = carry_ref[hi:, :]
            carry = jnp.concatenate([carry] * (tile // LANES), axis=1)
            arg = jnp.concatenate([arg[:tile], arg[tile:] + carry], axis=0)
        w = jnp.exp2(arg)
        w_diag = jnp.where(causal, w[:tile], 0.0)
        w = w_diag if first else jnp.concatenate([w_diag, w[tile:]], axis=0)
        pv = jnp.dot(w.astype(BF16), v, preferred_element_type=F32)
        total = -jnp.broadcast_to(jnp.sum(sp, axis=1, keepdims=True), (sp.shape[0], LANES))
        acc_ref[lo:hi, :] = pv[:tile]
        carry_ref[lo:hi, :] = total[:tile]
        if not first:
            acc_ref[hi:, :] += pv[tile:]
            carry_ref[hi:, :] += total[tile:]
    o_ref[...] = (acc_ref[...] * g_ref[...].astype(F32)).astype(o_ref.dtype)


def _attention(q, k, v, gate, *, batch, seq):
    m, d = q.shape
    heads = d // HEAD_DIM
    tile = min(256, seq)
    r3 = lambda a: a.reshape(batch, seq, d)
    spec = pl.BlockSpec((None, seq, HEAD_DIM), lambda b, h: (b, 0, h))
    est = 10 * seq * HEAD_DIM * 2 + 2 * seq * LANES * 4 + 8 * seq * tile * 4
    out = pl.pallas_call(
        functools.partial(_attn_kernel, tile=tile),
        out_shape=jax.ShapeDtypeStruct((batch, seq, d), BF16),
        grid=(batch, heads),
        in_specs=[spec, spec, spec, spec],
        out_specs=spec,
        scratch_shapes=[pltpu.VMEM((seq, HEAD_DIM), F32),
                        pltpu.VMEM((seq, LANES), F32)],
        compiler_params=pltpu.CompilerParams(
            dimension_semantics=("parallel", "parallel"),
            vmem_limit_bytes=_vmem_limit(est + (8 << 20))),
        name="stickbreak_attn",
    )(r3(q), r3(k), r3(v), r3(gate))
    return out.reshape(m, d)


def _ssm_prep_kernel(p_ref, bre_ref, bim_ref, cre_ref, cim_ref, t_ref, w_ref, vt_ref, lam_ref):
    n_rows = 24
    n = lax.broadcasted_iota(jnp.int32, (n_rows, LANES), 0).astype(F32)
    lane = lax.broadcasted_iota(jnp.int32, (GROUP, LANES), 1)

    def per_group(gi, _):
        a_re = p_ref[gi, 0:1, :]
        a_im = p_ref[gi, 1:2, :]
        dt = jnp.exp(p_ref[gi, 2:3, :])
        half_mask = p_ref[gi, 3:4, :]
        mag = jnp.exp(n * (a_re * dt))
        ang = n * (a_im * dt)
        pre = mag * jnp.cos(ang)
        pim = mag * jnp.sin(ang)
        lam_re = pre[1:2, :]
        lam_im = pim[1:2, :]
        den = a_re * a_re + a_im * a_im
        f_re = ((lam_re - 1.0) * a_re + lam_im * a_im) / den
        f_im = (lam_im * a_re - (lam_re - 1.0) * a_im) / den
        fre = f_re * pre - f_im * pim
        fim = f_re * pim + f_im * pre
        bre = bre_ref[gi]
        bim = bim_ref[gi]
        cre = cre_ref[gi]
        cim = cim_ref[gi]

        v_blocks = []
        for e in range(CHUNK + 1):
            va = cre * pre[e:e + 1, :] - cim * pim[e:e + 1, :]
            vb = -(cre * pim[e:e + 1, :] + cim * pre[e:e + 1, :])
            v_blocks.append(jnp.concatenate([va, vb], axis=1))
        for t in range(CHUNK):
            vt_ref[gi, t * GROUP:(t + 1) * GROUP, :] = v_blocks[t + 1].astype(vt_ref.dtype)
        v0t = jnp.concatenate(v_blocks[:CHUNK], axis=0)

        for s in range(CHUNK):
            e = CHUNK - 1 - s
            wa = fre[e:e + 1, :] * bre - fim[e:e + 1, :] * bim
            wb = fre[e:e + 1, :] * bim + fim[e:e + 1, :] * bre
            w_ref[gi, s * GROUP:(s + 1) * GROUP, :] = jnp.concatenate([wa, wb], axis=1).astype(w_ref.dtype)

        wb0 = jnp.concatenate([fre[0:1, :] * bre - fim[0:1, :] * bim,
                               fre[0:1, :] * bim + fim[0:1, :] * bre], axis=1)
        kmat = lax.dot_general(wb0, v0t, (((1,), (1,)), ((), ())),
                               precision=lax.Precision.HIGHEST, preferred_element_type=F32)
        k0 = kmat[:, :LANES]
        k1 = kmat[:, LANES:]
        zero = jnp.zeros_like(k0)
        for s in range(CHUNK):
            sh = s * GROUP
            if s == 0:
                lo, hi = k0, k1
            elif sh < LANES:
                r0 = pltpu.roll(k0, sh, 1)
                r1 = pltpu.roll(k1, sh, 1)
                lo = jnp.where(lane >= sh, r0, 0.0)
                hi = jnp.where(lane >= sh, r1, r0)
            elif sh == LANES:
                lo, hi = zero, k0
            else:
                r0 = pltpu.roll(k0, sh - LANES, 1)
                lo = zero
                hi = jnp.where(lane >= sh - LANES, r0, 0.0)
            t_ref[gi, s * GROUP:(s + 1) * GROUP, :] = jnp.concatenate([lo, hi], axis=1).astype(t_ref.dtype)

        lam_rows = jnp.concatenate([pre[CHUNK:CHUNK + 1, :] * half_mask, pim[CHUNK:CHUNK + 1, :] * half_mask,
                                    jnp.zeros((6, LANES), F32)], axis=0)
        lam_ref[gi] = lam_rows
        return 0

    lax.fori_loop(0, p_ref.shape[0], per_group, 0)


def _ssm_prep(a_re, a_im, log_dt, b_re, b_im, c_re, c_im):
    g = a_re.shape[0]
    even = (jnp.arange(g) % 2 == 0)[:, None, None]

    def pad_half(a):
        z = jnp.zeros_like(a)
        return jnp.where(even, jnp.concatenate([a, z], -1), jnp.concatenate([z, a], -1))

    dup = lambda a: jnp.concatenate([a, a], -1)
    half_mask = jnp.where(even[:, 0], jnp.concatenate([jnp.ones((g, STATE)), jnp.zeros((g, STATE))], -1),
                          jnp.concatenate([jnp.zeros((g, STATE)), jnp.ones((g, STATE))], -1))
    params = jnp.stack([dup(a_re), dup(a_im), jnp.broadcast_to(log_dt[:, None], (g, LANES)), half_mask]
                       + [jnp.zeros((g, LANES), F32)] * 4, axis=1).astype(F32)
    bt_re = pad_half(jnp.swapaxes(b_re, 1, 2))
    bt_im = pad_half(jnp.swapaxes(b_im, 1, 2))
    cp_re = pad_half(c_re)
    cp_im = pad_half(c_im)
    gb = GROUPS_PER_TILE
    kk = CHUNK * GROUP
    spec8 = pl.BlockSpec((gb, 8, LANES), lambda i: (i, 0, 0))
    spec16 = pl.BlockSpec((gb, GROUP, LANES), lambda i: (i, 0, 0))
    specm = pl.BlockSpec((gb, kk, kk), lambda i: (i, 0, 0))
    return pl.pallas_call(
        _ssm_prep_kernel,
        out_shape=(jax.ShapeDtypeStruct((g, kk, kk), BF16),
                   jax.ShapeDtypeStruct((g, kk, 2 * LANES), BF16),
                   jax.ShapeDtypeStruct((g, kk, 2 * LANES), BF16),
                   jax.ShapeDtypeStruct((g, 8, LANES), F32)),
        grid=(g // gb,),
        in_specs=[spec8, spec16, spec16, spec16, spec16],
        out_specs=(specm, specm, specm, spec8),
        compiler_params=pltpu.CompilerParams(dimension_semantics=("parallel",)),
        name="ssm_prep",
    )(params, bt_re, bt_im, cp_re, cp_im)


def _ssm_kernel(u_ref, t_ref, w_ref, vt_ref, lam_ref, d_ref, o_ref, zs, ys, ss, hs, yscr, *, batch, nchunk):
    rows_per_step = 32
    n_rows = batch * nchunk
    pairs = GROUPS_PER_TILE // 2
    lane_group = lax.broadcasted_iota(jnp.int32, (rows_per_step, LANES), 1) // GROUP
    bit_set = {d: (lane_group & d) != 0 for d in (4, 2, 1)}

    def transpose_pieces(xs):
        for d in (4, 2, 1):
            new = list(xs)
            for m in range(GROUPS_PER_TILE):
                if m & d:
                    continue
                a, b = xs[m], xs[m + d]
                new[m] = jnp.where(bit_set[d], pltpu.roll(b, GROUP * d, 1), a)
                new[m + d] = jnp.where(bit_set[d], b, pltpu.roll(a, LANES - GROUP * d, 1))
            xs = new
        return xs

    def fold(rb, _):
        r0 = pl.multiple_of(rb * rows_per_step, rows_per_step)
        for half in range(2):
            xs = [u_ref[pl.ds(r0 * CHUNK + half * 8 + m, rows_per_step, stride=CHUNK), :]
                  for m in range(GROUPS_PER_TILE)]
            pieces = transpose_pieces(xs)
            for g in range(GROUPS_PER_TILE):
                zs[g, pl.ds(r0, rows_per_step), half * LANES:(half + 1) * LANES] = pieces[g].astype(zs.dtype)
        return 0

    lax.fori_loop(0, n_rows // rows_per_step, fold, 0)

    for p in range(pairs):
        s_end = (jnp.dot(zs[2 * p], w_ref[2 * p], preferred_element_type=F32)
                 + jnp.dot(zs[2 * p + 1], w_ref[2 * p + 1], preferred_element_type=F32))
        ss[2 * p] = s_end[:, :LANES]
        ss[2 * p + 1] = s_end[:, LANES:]

    lre = [lam_ref[2 * p, 0:1, :] + lam_ref[2 * p + 1, 0:1, :] for p in range(pairs)]
    lim = [lam_ref[2 * p, 1:2, :] + lam_ref[2 * p + 1, 1:2, :] for p in range(pairs)]

    def scan(c, carry):
        new = []
        for p in range(pairs):
            h_re, h_im = carry[2 * p], carry[2 * p + 1]
            rows = pl.ds(c, batch, stride=nchunk)
            hs[2 * p, rows, :] = h_re
            hs[2 * p + 1, rows, :] = h_im
            s_re = ss[2 * p, rows, :]
            s_im = ss[2 * p + 1, rows, :]
            new.append(h_re * lre[p] - h_im * lim[p] + s_re)
            new.append(h_im * lre[p] + h_re * lim[p] + s_im)
        return tuple(new)

    zero_state = jnp.zeros((batch, LANES), F32)
    lax.fori_loop(0, nchunk, scan, tuple(zero_state for _ in range(2 * pairs)))

    for g in range(GROUPS_PER_TILE):
        h_in = jnp.concatenate([hs[2 * (g // 2)], hs[2 * (g // 2) + 1]], axis=1).astype(BF16)
        ys[g] = (jnp.dot(zs[g], t_ref[g], preferred_element_type=F32)
                 + lax.dot_general(h_in, vt_ref[g], (((1,), (1,)), ((), ())), preferred_element_type=F32))

    d_row = d_ref[...]

    def unfold(rb, _):
        r0 = pl.multiple_of(rb * rows_per_step, rows_per_step)
        for half in range(2):
            xs = [ys[g, pl.ds(r0, rows_per_step), half * LANES:(half + 1) * LANES]
                  for g in range(GROUPS_PER_TILE)]
            pieces = transpose_pieces(xs)
            for m in range(GROUPS_PER_TILE):
                rows = pl.ds(r0 * CHUNK + half * 8 + m, rows_per_step, stride=CHUNK)
                yscr[rows, :] = pieces[m] + d_row * u_ref[rows, :]
        return 0

    lax.fori_loop(0, n_rows // rows_per_step, unfold, 0)

    dense_rows = 256

    def activate(i, _):
        rows = pl.ds(pl.multiple_of(i * dense_rows, dense_rows), dense_rows)
        o_ref[rows, :] = jax.nn.gelu(yscr[rows, :]).astype(o_ref.dtype)
        return 0

    lax.fori_loop(0, (n_rows * CHUNK) // dense_rows, activate, 0)


def _ssm(u, tmat, wmat, vtmat, lam, d_vec, *, batch, seq):
    m, d = u.shape
    nchunk = seq // CHUNK
    n_rows = batch * nchunk
    gb = GROUPS_PER_TILE
    kk = CHUNK * GROUP
    specm = pl.BlockSpec((gb, kk, kk), lambda i: (i, 0, 0))
    est = (2 * (m * LANES * 4 + m * LANES * 2 + 3 * gb * kk * kk * 2)
           + gb * n_rows * kk * (2 + 4) + 2 * (gb // 2) * n_rows * kk * 4 + m * LANES * 4)
    return pl.pallas_call(
        functools.partial(_ssm_kernel, batch=batch, nchunk=nchunk),
        out_shape=jax.ShapeDtypeStruct((m, d), BF16),
        grid=(d // LANES,),
        in_specs=[pl.BlockSpec((m, LANES), lambda i: (0, i)),
                  specm, specm, specm,
                  pl.BlockSpec((gb, 8, LANES), lambda i: (i, 0, 0)),
                  pl.BlockSpec((1, LANES), lambda i: (0, i))],
        out_specs=pl.BlockSpec((m, LANES), lambda i: (0, i)),
        scratch_shapes=[pltpu.VMEM((gb, n_rows, kk), BF16),
                        pltpu.VMEM((gb, n_rows, kk), F32),
                        pltpu.VMEM((gb, n_rows, LANES), F32),
                        pltpu.VMEM((gb, n_rows, LANES), F32),
                        pltpu.VMEM((m, LANES), F32)],
        compiler_params=pltpu.CompilerParams(
            dimension_semantics=("parallel",),
            vmem_limit_bytes=_vmem_limit(est + (4 << 20))),
        name="s5_scan",
    )(u, tmat, wmat, vtmat, lam, d_vec.reshape(1, d))


def kernel(x, norm_g, attn_w_in, attn_q_g, attn_k_g, attn_w_out, ssm_w_in, ssm_A_re, ssm_A_im, ssm_log_dt,
           ssm_B_re, ssm_B_im, ssm_C_re, ssm_C_im, ssm_D, ssm_glu_w, ssm_glu_b, ssm_w_out):
    batch, seq, d = x.shape
    m = batch * seq
    assert d % LANES == 0 and seq % 256 == 0 and m % 256 == 0
    assert ssm_A_re.shape[1:] == (d // GROUP, STATE) and (d // GROUP) % GROUPS_PER_TILE == 0
    x2 = x.reshape(m, d)

    h = _rmsnorm(x2, norm_g[0])
    w_in = attn_w_in[0]
    qg = attn_q_g[0].reshape(1, HEAD_DIM)
    kg = attn_k_g[0].reshape(1, HEAD_DIM)
    q = _matmul(h, w_in, col_off=0, n=d, mode="headnorm", out_dtype=BF16, extras=(qg,),
                scale=LOG2E / math.sqrt(HEAD_DIM))
    k = _matmul(h, w_in, col_off=d, n=d, mode="headnorm", out_dtype=BF16, extras=(kg,))
    v = _matmul(h, w_in, col_off=2 * d, n=d, mode="store", out_dtype=BF16)
    gate = _matmul(h, w_in, col_off=3 * d, n=d, mode="silu", out_dtype=BF16)
    og = _attention(q, k, v, gate, batch=batch, seq=seq)
    x2 = _matmul(og, attn_w_out[0], col_off=0, n=d, mode="residual", out_dtype=F32, extras=(x2,))

    h = _rmsnorm(x2, norm_g[1])
    w_in = ssm_w_in[0]
    u = _matmul(h, w_in, col_off=0, n=d, mode="store", out_dtype=F32)
    gate = _matmul(h, w_in, col_off=d, n=d, mode="silu", out_dtype=BF16)
    tmat, wmat, vtmat, lam = _ssm_prep(ssm_A_re[0], ssm_A_im[0], ssm_log_dt[0], ssm_B_re[0], ssm_B_im[0],
                                       ssm_C_re[0], ssm_C_im[0])
    y = _ssm(u, tmat, wmat, vtmat, lam, ssm_D[0], batch=batch, seq=seq)
    yg = _matmul(y, ssm_glu_w[0], col_off=0, n=d, mode="glu", out_dtype=BF16,
                 extras=(ssm_glu_b[0].reshape(1, d), y, gate))
    x2 = _matmul(yg, ssm_w_out[0], col_off=0, n=d, mode="residual", out_dtype=F32, extras=(x2,))
    return x2.reshape(batch, seq, d)
```

```python
import functools
import math

import jax
import jax.numpy as jnp
from jax import lax
from jax.experimental import pallas as pl
from jax.experimental.pallas import tpu as pltpu

HEAD_DIM = 128
GROUP = 16
STATE = 64
RMS_EPS = 1e-6
LANES = 128
CHUNK = 16
LOG2E = 1.4426950408889634
SCAN_PITCH_PAD = 8
GROUPS_PER_TILE = LANES // GROUP
VMEM_CAP_BYTES = 60 * 1024 * 1024

F32 = jnp.float32
BF16 = jnp.bfloat16


def _vmem_limit(estimate_bytes):
    return int(min(VMEM_CAP_BYTES, max(32 * 1024 * 1024, estimate_bytes)))


def _rmsnorm_kernel(x_ref, g_ref, o_ref):
    x = x_ref[...]
    r = lax.rsqrt(jnp.mean(x * x, axis=-1, keepdims=True) + RMS_EPS)
    o_ref[...] = ((x * r) * g_ref[...]).astype(o_ref.dtype)


def _rmsnorm(x2, g):
    m, d = x2.shape
    tm = min(256, m)
    return pl.pallas_call(
        _rmsnorm_kernel,
        out_shape=jax.ShapeDtypeStruct((m, d), BF16),
        grid=(m // tm,),
        in_specs=[pl.BlockSpec((tm, d), lambda i: (i, 0)),
                  pl.BlockSpec((1, d), lambda i: (0, 0))],
        out_specs=pl.BlockSpec((tm, d), lambda i: (i, 0)),
        compiler_params=pltpu.CompilerParams(dimension_semantics=("parallel",)),
        name="rmsnorm",
    )(x2, g.reshape(1, d))


ROWS = "rows"
POS = "pos"
NAT_BY_POS = "nat_by_pos"


def _tile_specs(arr, layout, *, tm, width, col):
    if layout == ROWS:
        return [arr], [pl.BlockSpec((tm, width), lambda j, i: (i, col(j)))]
    if layout == POS:
        rows = arr.shape[1]
        return [arr], [pl.BlockSpec((tm // rows, rows, width), lambda j, i: (i, 0, col(j)))]
    m, w = arr.shape
    rows = m // CHUNK
    per_pos = w // width
    n_pos = tm // rows
    view = arr.reshape(rows, CHUNK * w)
    specs = [pl.BlockSpec((rows, width), functools.partial(
        lambda j, i, q: (0, (n_pos * i + q) * per_pos + col(j)), q=q)) for q in range(n_pos)]
    return [view] * n_pos, specs


def _stack_rows(refs):
    vals = [r[...] for r in refs]
    vals = [v.reshape(-1, v.shape[-1]) if v.ndim == 3 else v for v in vals]
    return vals[0] if len(vals) == 1 else jnp.concatenate(vals, axis=0)


def _mm_kernel(*refs, mode, scale, x_parts, extra_parts):
    x = _stack_rows(refs[:x_parts])
    w_ref = refs[x_parts]
    extras, at = [], x_parts + 1
    for n_parts in extra_parts:
        extras.append(refs[at:at + n_parts])
        at += n_parts
    o_ref = refs[at]
    acc = jnp.dot(x, w_ref[...].astype(BF16), preferred_element_type=F32)

    def emit(val):
        o_ref[...] = val.astype(o_ref.dtype).reshape(o_ref.shape)

    if mode == "headnorm":
        g = extras[0][0][...]
        for h in range(acc.shape[1] // HEAD_DIM):
            sl = slice(h * HEAD_DIM, (h + 1) * HEAD_DIM)
            a = acc[:, sl]
            r = lax.rsqrt(jnp.mean(a * a, axis=-1, keepdims=True) + RMS_EPS)
            o_ref[:, sl] = (((a * r) * g) * scale).astype(o_ref.dtype)
    elif mode == "silu":
        emit(acc * jax.nn.sigmoid(acc))
    elif mode == "residual":
        emit(_stack_rows(extras[0]) + acc)
    elif mode == "glu":
        t = acc + extras[0][0][...]
        emit(_stack_rows(extras[1]).astype(F32) * jax.nn.sigmoid(t) * _stack_rows(extras[2]).astype(F32))
    else:
        emit(acc)


def _matmul(x, w, *, col_off, n, mode, out_dtype, extras=(), scale=1.0, tm=1024, tn=512,
            x_layout=ROWS, out_layout=ROWS):
    k = w.shape[0]
    m = x.size // k
    tm = min(tm, m)
    tn = min(tn, n)
    off = col_off // tn
    x_views, in_specs = _tile_specs(x, x_layout, tm=tm, width=k, col=lambda j: 0)
    operands = list(x_views) + [w]
    x_parts = len(x_views)
    in_specs = in_specs + [pl.BlockSpec((k, tn), lambda j, i: (0, j + off))]
    extra_parts = []
    extra_bytes = 0
    for e, layout in extras:
        if e.ndim == 2 and e.shape[0] == 1:
            width = e.shape[1]
            spec = (pl.BlockSpec((1, tn), lambda j, i: (0, j)) if width == n
                    else pl.BlockSpec((1, width), lambda j, i: (0, 0)))
            views, specs = [e], [spec]
        else:
            views, specs = _tile_specs(e, layout, tm=tm, width=tn, col=lambda j: j)
            extra_bytes += tm * tn * e.dtype.itemsize
        operands += views
        in_specs += specs
        extra_parts.append(len(views))
    if out_layout == ROWS:
        out_shape = jax.ShapeDtypeStruct((m, n), out_dtype)
        out_spec = pl.BlockSpec((tm, tn), lambda j, i: (i, j))
    elif out_layout == POS:
        rows = m // CHUNK
        out_shape = jax.ShapeDtypeStruct((CHUNK, rows, n), out_dtype)
        out_spec = pl.BlockSpec((tm // rows, rows, tn), lambda j, i: (i, 0, j))
    else:
        rows = m // CHUNK
        assert tm == rows
        out_shape = jax.ShapeDtypeStruct((rows, CHUNK * n), out_dtype)
        out_spec = pl.BlockSpec((rows, tn), lambda j, i: (0, i * (n // tn) + j))
    out_bytes = tm * tn * jnp.dtype(out_dtype).itemsize
    est = (2 * (tm * k * 2 + k * tn * w.dtype.itemsize + out_bytes + extra_bytes)
           + k * tn * 2 + 3 * tm * tn * 4)
    out = pl.pallas_call(
        functools.partial(_mm_kernel, mode=mode, scale=scale, x_parts=x_parts, extra_parts=tuple(extra_parts)),
        out_shape=out_shape,
        grid=(n // tn, m // tm),
        in_specs=in_specs,
        out_specs=out_spec,
        compiler_params=pltpu.CompilerParams(
            dimension_semantics=("parallel", "parallel"),
            vmem_limit_bytes=_vmem_limit(est + (4 << 20))),
        name="mm_" + mode,
    )(*operands)
    return out.reshape(m, n) if out_layout == NAT_BY_POS else out


def _attn_kernel(q_ref, k_ref, v_ref, g_ref, o_ref, acc_ref, carry_ref, *, tile):
    s_len = q_ref.shape[0]
    nt = s_len // tile
    row = lax.broadcasted_iota(jnp.int32, (tile, tile), 0)
    col = lax.broadcasted_iota(jnp.int32, (tile, tile), 1)
    causal = col < row
    neg_later = jnp.where(row > col, -1.0, 0.0).astype(BF16)
    for j in reversed(range(nt)):
        first = j == nt - 1
        lo, hi = j * tile, (j + 1) * tile
        k = k_ref[lo:hi, :]
        v = v_ref[lo:hi, :]
        z = lax.dot_general(q_ref[lo:, :], k, (((1,), (1,)), ((), ())), preferred_element_type=F32)
        sp = jnp.maximum(z, 0.0) + jnp.log2(1.0 + jnp.exp2(-jnp.abs(z)))
        sp_diag = jnp.where(causal, sp[:tile], 0.0)
        sp = sp_diag if first else jnp.concatenate([sp_diag, sp[tile:]], axis=0)
        suffix = jnp.dot(sp.astype(BF16), neg_later, preferred_element_type=F32)
        arg = (z - sp) + suffix
        if not first:
            carry = carry_ref[hi:, :]
            carry = jnp.concatenate([carry] * (tile // LANES), axis=1)
            arg = jnp.concatenate([arg[:tile], arg[tile:] + carry], axis=0)
        w = jnp.exp2(arg)
        w_diag = jnp.where(causal, w[:tile], 0.0)
        w = w_diag if first else jnp.concatenate([w_diag, w[tile:]], axis=0)
        pv = jnp.dot(w.astype(BF16), v, preferred_element_type=F32)
        total = -jnp.broadcast_to(jnp.sum(sp, axis=1, keepdims=True), (sp.shape[0], LANES))
        acc_ref[lo:hi, :] = pv[:tile]
        carry_ref[lo:hi, :] = total[:tile]
        if not first:
            acc_ref[hi:, :] += pv[tile:]
            carry_ref[hi:, :] += total[tile:]
    o_ref[...] = (acc_ref[...] * g_ref[...].astype(F32)).astype(o_ref.dtype)


def _attention(q, k, v, gate, *, batch, seq):
    m, d = q.shape
    heads = d // HEAD_DIM
    tile = min(256, seq)
    r3 = lambda a: a.reshape(batch, seq, d)
    spec = pl.BlockSpec((None, seq, HEAD_DIM), lambda b, h: (b, 0, h))
    est = 10 * seq * HEAD_DIM * 2 + 2 * seq * LANES * 4 + 8 * seq * tile * 4
    out = pl.pallas_call(
        functools.partial(_attn_kernel, tile=tile),
        out_shape=jax.ShapeDtypeStruct((batch, seq, d), BF16),
        grid=(batch, heads),
        in_specs=[spec, spec, spec, spec],
        out_specs=spec,
        scratch_shapes=[pltpu.VMEM((seq, HEAD_DIM), F32),
                        pltpu.VMEM((seq, LANES), F32)],
        compiler_params=pltpu.CompilerParams(
            dimension_semantics=("parallel", "parallel"),
            vmem_limit_bytes=_vmem_limit(est + (8 << 20))),
        name="stickbreak_attn",
    )(r3(q), r3(k), r3(v), r3(gate))
    return out.reshape(m, d)


def _ssm_prep_kernel(p_ref, bre_ref, bim_ref, cre_ref, cim_ref, t_ref, w_ref, vt_ref, lam_ref):
    n_rows = 24
    n = lax.broadcasted_iota(jnp.int32, (n_rows, LANES), 0).astype(F32)
    lane = lax.broadcasted_iota(jnp.int32, (GROUP, LANES), 1)

    def per_group(gi, _):
        a_re = p_ref[gi, 0:1, :]
        a_im = p_ref[gi, 1:2, :]
        dt = jnp.exp(p_ref[gi, 2:3, :])
        half_mask = p_ref[gi, 3:4, :]
        mag = jnp.exp(n * (a_re * dt))
        ang = n * (a_im * dt)
        pre = mag * jnp.cos(ang)
        pim = mag * jnp.sin(ang)
        lam_re = pre[1:2, :]
        lam_im = pim[1:2, :]
        den = a_re * a_re + a_im * a_im
        f_re = ((lam_re - 1.0) * a_re + lam_im * a_im) / den
        f_im = (lam_im * a_re - (lam_re - 1.0) * a_im) / den
        fre = f_re * pre - f_im * pim
        fim = f_re * pim + f_im * pre
        bre = bre_ref[gi]
        bim = bim_ref[gi]
        cre = cre_ref[gi]
        cim = cim_ref[gi]

        v_blocks = []
        for e in range(CHUNK + 1):
            va = cre * pre[e:e + 1, :] - cim * pim[e:e + 1, :]
            vb = -(cre * pim[e:e + 1, :] + cim * pre[e:e + 1, :])
            v_blocks.append(jnp.concatenate([va, vb], axis=1))
        for t in range(CHUNK):
            vt_ref[gi, t * GROUP:(t + 1) * GROUP, :] = v_blocks[t + 1].astype(vt_ref.dtype)
        v0t = jnp.concatenate(v_blocks[:CHUNK], axis=0)

        for s in range(CHUNK):
            e = CHUNK - 1 - s
            wa = fre[e:e + 1, :] * bre - fim[e:e + 1, :] * bim
            wb = fre[e:e + 1, :] * bim + fim[e:e + 1, :] * bre
            w_ref[gi, s * GROUP:(s + 1) * GROUP, :] = jnp.concatenate([wa, wb], axis=1).astype(w_ref.dtype)

        wb0 = jnp.concatenate([fre[0:1, :] * bre - fim[0:1, :] * bim,
                               fre[0:1, :] * bim + fim[0:1, :] * bre], axis=1)
        kmat = lax.dot_general(wb0, v0t, (((1,), (1,)), ((), ())),
                               precision=lax.Precision.HIGHEST, preferred_element_type=F32)
        k0 = kmat[:, :LANES]
        k1 = kmat[:, LANES:]
        zero = jnp.zeros_like(k0)
        for s in range(CHUNK):
            sh = s * GROUP
            if s == 0:
                lo, hi = k0, k1
            elif sh < LANES:
                r0 = pltpu.roll(k0, sh, 1)
                r1 = pltpu.roll(k1, sh, 1)
                lo = jnp.where(lane >= sh, r0, 0.0)
                hi = jnp.where(lane >= sh, r1, r0)
            elif sh == LANES:
                lo, hi = zero, k0
            else:
                r0 = pltpu.roll(k0, sh - LANES, 1)
                lo = zero
                hi = jnp.where(lane >= sh - LANES, r0, 0.0)
            t_ref[gi, s * GROUP:(s + 1) * GROUP, :] = jnp.concatenate([lo, hi], axis=1).astype(t_ref.dtype)

        lam_rows = jnp.concatenate([pre[CHUNK:CHUNK + 1, :] * half_mask, pim[CHUNK:CHUNK + 1, :] * half_mask,
                                    jnp.zeros((6, LANES), F32)], axis=0)
        lam_ref[gi] = lam_rows
        return 0

    lax.fori_loop(0, p_ref.shape[0], per_group, 0, unroll=True)


def _ssm_prep(a_re, a_im, log_dt, b_re, b_im, c_re, c_im):
    g = a_re.shape[0]
    even = (jnp.arange(g) % 2 == 0)[:, None, None]

    def pad_half(a):
        z = jnp.zeros_like(a)
        return jnp.where(even, jnp.concatenate([a, z], -1), jnp.concatenate([z, a], -1))

    dup = lambda a: jnp.concatenate([a, a], -1)
    half_mask = jnp.where(even[:, 0], jnp.concatenate([jnp.ones((g, STATE)), jnp.zeros((g, STATE))], -1),
                          jnp.concatenate([jnp.zeros((g, STATE)), jnp.ones((g, STATE))], -1))
    params = jnp.stack([dup(a_re), dup(a_im), jnp.broadcast_to(log_dt[:, None], (g, LANES)), half_mask]
                       + [jnp.zeros((g, LANES), F32)] * 4, axis=1).astype(F32)
    bt_re = pad_half(jnp.swapaxes(b_re, 1, 2))
    bt_im = pad_half(jnp.swapaxes(b_im, 1, 2))
    cp_re = pad_half(c_re)
    cp_im = pad_half(c_im)
    gb = GROUPS_PER_TILE
    kk = CHUNK * GROUP
    spec8 = pl.BlockSpec((gb, 8, LANES), lambda i: (i, 0, 0))
    spec16 = pl.BlockSpec((gb, GROUP, LANES), lambda i: (i, 0, 0))
    specm = pl.BlockSpec((gb, kk, kk), lambda i: (i, 0, 0))
    return pl.pallas_call(
        _ssm_prep_kernel,
        out_shape=(jax.ShapeDtypeStruct((g, kk, kk), BF16),
                   jax.ShapeDtypeStruct((g, kk, 2 * LANES), BF16),
                   jax.ShapeDtypeStruct((g, kk, 2 * LANES), BF16),
                   jax.ShapeDtypeStruct((g, 8, LANES), F32)),
        grid=(g // gb,),
        in_specs=[spec8, spec16, spec16, spec16, spec16],
        out_specs=(specm, specm, specm, spec8),
        compiler_params=pltpu.CompilerParams(dimension_semantics=("parallel",)),
        name="ssm_prep",
    )(params, bt_re, bt_im, cp_re, cp_im)


def _ssm_kernel(u_ref, t_ref, w_ref, vt_ref, lam_ref, d_ref, o_ref, zs, ys, ss, hs, *, batch, nchunk, block_rows):
    n_rows = batch * nchunk
    pairs = GROUPS_PER_TILE // 2
    pitch = nchunk + SCAN_PITCH_PAD

    def lane_masks(rows):
        lane_group = lax.broadcasted_iota(jnp.int32, (rows, LANES), 1) // GROUP
        return {d: (lane_group & d) != 0 for d in (4, 2, 1)}

    def transpose_pieces(xs, bit_set):
        for d in (4, 2, 1):
            new = list(xs)
            for m in range(GROUPS_PER_TILE):
                if m & d:
                    continue
                a, b = xs[m], xs[m + d]
                new[m] = jnp.where(bit_set[d], pltpu.roll(b, GROUP * d, 1), a)
                new[m + d] = jnp.where(bit_set[d], b, pltpu.roll(a, LANES - GROUP * d, 1))
            xs = new
        return xs

    masks = lane_masks(block_rows)
    masks_packed = lane_masks(block_rows // 2)
    row_blocks = [slice(r, r + block_rows) for r in range(0, n_rows, block_rows)]

    for rows in row_blocks:
        for half in range(2):
            xs = [pltpu.bitcast(u_ref[half * 8 + m, rows, :].astype(BF16), jnp.uint32)
                  for m in range(GROUPS_PER_TILE)]
            pieces = transpose_pieces(xs, masks_packed)
            for g in range(GROUPS_PER_TILE):
                zs[g, rows, half * LANES:(half + 1) * LANES] = pltpu.bitcast(pieces[g], BF16)

    for p in range(pairs):
        s_end = (jnp.dot(zs[2 * p], w_ref[2 * p], preferred_element_type=F32)
                 + jnp.dot(zs[2 * p + 1], w_ref[2 * p + 1], preferred_element_type=F32))
        for b in range(batch):
            ss[2 * p, b * pitch:b * pitch + nchunk, :] = s_end[b * nchunk:(b + 1) * nchunk, :LANES]
            ss[2 * p + 1, b * pitch:b * pitch + nchunk, :] = s_end[b * nchunk:(b + 1) * nchunk, LANES:]

    lre = [lam_ref[2 * p, 0:1, :] + lam_ref[2 * p + 1, 0:1, :] for p in range(pairs)]
    lim = [lam_ref[2 * p, 1:2, :] + lam_ref[2 * p + 1, 1:2, :] for p in range(pairs)]

    def scan(c, carry):
        new = []
        for p in range(pairs):
            h_re, h_im = carry[2 * p], carry[2 * p + 1]
            rows = pl.ds(c, batch, stride=pitch)
            hs[2 * p, rows, :] = h_re
            hs[2 * p + 1, rows, :] = h_im
            s_re = ss[2 * p, rows, :]
            s_im = ss[2 * p + 1, rows, :]
            new.append(h_re * lre[p] - h_im * lim[p] + s_re)
            new.append(h_im * lre[p] + h_re * lim[p] + s_im)
        return tuple(new)

    zero_state = jnp.zeros((batch, LANES), F32)
    lax.fori_loop(0, nchunk, scan, tuple(zero_state for _ in range(2 * pairs)))

    def entering(q):
        return jnp.concatenate([hs[q, b * pitch:b * pitch + nchunk, :] for b in range(batch)], axis=0)

    for g in range(GROUPS_PER_TILE):
        h_in = jnp.concatenate([entering(2 * (g // 2)), entering(2 * (g // 2) + 1)], axis=1).astype(BF16)
        ys[g] = (jnp.dot(zs[g], t_ref[g], preferred_element_type=F32)
                 + lax.dot_general(h_in, vt_ref[g], (((1,), (1,)), ((), ())), preferred_element_type=F32))

    d_row = d_ref[...]
    for rows in row_blocks:
        for half in range(2):
            xs = [ys[g, rows, half * LANES:(half + 1) * LANES] for g in range(GROUPS_PER_TILE)]
            pieces = transpose_pieces(xs, masks)
            for m in range(GROUPS_PER_TILE):
                t = half * 8 + m
                o_ref[t, rows, :] = jax.nn.gelu(pieces[m] + d_row * u_ref[t, rows, :]).astype(o_ref.dtype)


def _ssm(u3, tmat, wmat, vtmat, lam, d_vec, *, batch, seq):
    _, n_rows, d = u3.shape
    nchunk = seq // CHUNK
    n_pad = batch * (nchunk + SCAN_PITCH_PAD)
    gb = GROUPS_PER_TILE
    kk = CHUNK * GROUP
    block_rows = min(n_rows, 512)
    specm = pl.BlockSpec((gb, kk, kk), lambda i: (i, 0, 0))
    spec_u = pl.BlockSpec((CHUNK, n_rows, LANES), lambda i: (0, 0, i))
    est = (2 * CHUNK * n_rows * LANES * (4 + 2) + 2 * 3 * gb * kk * kk * 2
           + gb * n_rows * kk * (2 + 4) + 2 * gb * n_pad * LANES * 4 + 8 * n_rows * kk * 4)
    return pl.pallas_call(
        functools.partial(_ssm_kernel, batch=batch, nchunk=nchunk, block_rows=block_rows),
        out_shape=jax.ShapeDtypeStruct((CHUNK, n_rows, d), BF16),
        grid=(d // LANES,),
        in_specs=[spec_u, specm, specm, specm,
                  pl.BlockSpec((gb, 8, LANES), lambda i: (i, 0, 0)),
                  pl.BlockSpec((1, LANES), lambda i: (0, i))],
        out_specs=spec_u,
        scratch_shapes=[pltpu.VMEM((gb, n_rows, kk), BF16),
                        pltpu.VMEM((gb, n_rows, kk), F32),
                        pltpu.VMEM((gb, n_pad, LANES), F32),
                        pltpu.VMEM((gb, n_pad, LANES), F32)],
        compiler_params=pltpu.CompilerParams(
            dimension_semantics=("parallel",),
            vmem_limit_bytes=_vmem_limit(est)),
        name="s5_scan",
    )(u3, tmat, wmat, vtmat, lam, d_vec.reshape(1, d))


def kernel(x, norm_g, attn_w_in, attn_q_g, attn_k_g, attn_w_out, ssm_w_in, ssm_A_re, ssm_A_im, ssm_log_dt,
           ssm_B_re, ssm_B_im, ssm_C_re, ssm_C_im, ssm_D, ssm_glu_w, ssm_glu_b, ssm_w_out):
    batch, seq, d = x.shape
    m = batch * seq
    assert d % LANES == 0 and seq % 256 == 0 and (m // CHUNK) % 32 == 0
    assert ssm_A_re.shape[1:] == (d // GROUP, STATE) and (d // GROUP) % GROUPS_PER_TILE == 0
    x2 = x.reshape(m, d)

    h = _rmsnorm(x2, norm_g[0])
    w_in = attn_w_in[0]
    qg = attn_q_g[0].reshape(1, HEAD_DIM)
    kg = attn_k_g[0].reshape(1, HEAD_DIM)
    q = _matmul(h, w_in, col_off=0, n=d, mode="headnorm", out_dtype=BF16, extras=((qg, ROWS),),
                scale=LOG2E / math.sqrt(HEAD_DIM))
    k = _matmul(h, w_in, col_off=d, n=d, mode="headnorm", out_dtype=BF16, extras=((kg, ROWS),))
    v = _matmul(h, w_in, col_off=2 * d, n=d, mode="store", out_dtype=BF16)
    gate = _matmul(h, w_in, col_off=3 * d, n=d, mode="silu", out_dtype=BF16)
    og = _attention(q, k, v, gate, batch=batch, seq=seq)
    x2 = _matmul(og, attn_w_out[0], col_off=0, n=d, mode="residual", out_dtype=F32, extras=((x2, ROWS),))

    h = _rmsnorm(x2, norm_g[1])
    w_in = ssm_w_in[0]
    pos_tm = 2 * (m // CHUNK)
    u = _matmul(h, w_in, col_off=0, n=d, mode="store", out_dtype=F32, tm=pos_tm,
                x_layout=NAT_BY_POS, out_layout=POS)
    gate = _matmul(h, w_in, col_off=d, n=d, mode="silu", out_dtype=BF16, tm=pos_tm,
                   x_layout=NAT_BY_POS, out_layout=POS)
    tmat, wmat, vtmat, lam = _ssm_prep(ssm_A_re[0], ssm_A_im[0], ssm_log_dt[0], ssm_B_re[0], ssm_B_im[0],
                                       ssm_C_re[0], ssm_C_im[0])
    y = _ssm(u, tmat, wmat, vtmat, lam, ssm_D[0], batch=batch, seq=seq)
    yg = _matmul(y, ssm_glu_w[0], col_off=0, n=d, mode="glu", out_dtype=BF16, tm=pos_tm,
                 extras=((ssm_glu_b[0].reshape(1, d), ROWS), (y, POS), (gate, POS)),
                 x_layout=POS, out_layout=POS)
    x2 = _matmul(yg, ssm_w_out[0], col_off=0, n=d, mode="residual", out_dtype=F32, tm=m // CHUNK, tn=1024,
                 extras=((x2, NAT_BY_POS),), x_layout=POS, out_layout=NAT_BY_POS)
    return x2.reshape(batch, seq, d)
```

```python
import functools
import math

import jax
import jax.numpy as jnp
from jax import lax
from jax.experimental import pallas as pl
from jax.experimental.pallas import tpu as pltpu

HEAD_DIM = 128
GROUP = 16
STATE = 64
RMS_EPS = 1e-6
LANES = 128
CHUNK = 16
LOG2E = 1.4426950408889634
SCAN_PITCH_PAD = 8
GROUPS_PER_TILE = LANES // GROUP
VMEM_CAP_BYTES = 60 * 1024 * 1024

F32 = jnp.float32
BF16 = jnp.bfloat16


def _vmem_limit(estimate_bytes):
    return int(min(VMEM_CAP_BYTES, max(32 * 1024 * 1024, estimate_bytes)))


def _rmsnorm_kernel(x_ref, g_ref, o_ref):
    x = x_ref[...]
    r = lax.rsqrt(jnp.mean(x * x, axis=-1, keepdims=True) + RMS_EPS)
    o_ref[...] = ((x * r) * g_ref[...]).astype(o_ref.dtype)


def _rmsnorm(x2, g):
    m, d = x2.shape
    tm = min(256, m)
    return pl.pallas_call(
        _rmsnorm_kernel,
        out_shape=jax.ShapeDtypeStruct((m, d), BF16),
        grid=(m // tm,),
        in_specs=[pl.BlockSpec((tm, d), lambda i: (i, 0)),
                  pl.BlockSpec((1, d), lambda i: (0, 0))],
        out_specs=pl.BlockSpec((tm, d), lambda i: (i, 0)),
        compiler_params=pltpu.CompilerParams(dimension_semantics=("parallel",)),
        name="rmsnorm",
    )(x2, g.reshape(1, d))


def _mm_kernel(x_ref, w_ref, *rest, mode, scale):
    acc = jnp.dot(x_ref[...], w_ref[...].astype(BF16), preferred_element_type=F32)
    o_ref = rest[-1]
    if mode == "headnorm":
        g = rest[0][...]
        for h in range(acc.shape[1] // HEAD_DIM):
            sl = slice(h * HEAD_DIM, (h + 1) * HEAD_DIM)
            a = acc[:, sl]
            r = lax.rsqrt(jnp.mean(a * a, axis=-1, keepdims=True) + RMS_EPS)
            o_ref[:, sl] = (((a * r) * g) * scale).astype(o_ref.dtype)
    elif mode == "silu":
        o_ref[...] = (acc * jax.nn.sigmoid(acc)).astype(o_ref.dtype)
    elif mode == "residual":
        o_ref[...] = (rest[0][...] + acc).astype(o_ref.dtype)
    elif mode == "glu":
        b_ref, y_ref, gate_ref = rest[0], rest[1], rest[2]
        t = acc + b_ref[...]
        o_ref[...] = (y_ref[...].astype(F32) * jax.nn.sigmoid(t) * gate_ref[...].astype(F32)).astype(o_ref.dtype)
    else:
        o_ref[...] = acc.astype(o_ref.dtype)


def _matmul(x, w, *, col_off, n, mode, out_dtype, extras=(), scale=1.0, tm=512, tn=1024):
    m, k = x.shape
    tm = min(tm, m)
    tn = min(tn, n)
    off = col_off // tn
    in_specs = [pl.BlockSpec((tm, k), lambda j, i: (i, 0)),
                pl.BlockSpec((k, tn), lambda j, i: (0, j + off))]
    extra_bytes = 0
    for e in extras:
        if e.shape[0] == 1:
            width = e.shape[1]
            if width == n:
                in_specs.append(pl.BlockSpec((1, tn), lambda j, i: (0, j)))
            else:
                in_specs.append(pl.BlockSpec((1, width), lambda j, i: (0, 0)))
        else:
            in_specs.append(pl.BlockSpec((tm, tn), lambda j, i: (i, j)))
            extra_bytes += tm * tn * e.dtype.itemsize
    out_bytes = tm * tn * jnp.dtype(out_dtype).itemsize
    est = (2 * (tm * k * 2 + k * tn * w.dtype.itemsize + out_bytes + extra_bytes)
           + k * tn * 2 + 3 * tm * tn * 4)
    return pl.pallas_call(
        functools.partial(_mm_kernel, mode=mode, scale=scale),
        out_shape=jax.ShapeDtypeStruct((m, n), out_dtype),
        grid=(n // tn, m // tm),
        in_specs=in_specs,
        out_specs=pl.BlockSpec((tm, tn), lambda j, i: (i, j)),
        compiler_params=pltpu.CompilerParams(
            dimension_semantics=("parallel", "parallel"),
            vmem_limit_bytes=_vmem_limit(est + (4 << 20))),
        name="mm_" + mode,
    )(x, w, *extras)


def _attn_kernel(q_ref, k_ref, v_ref, g_ref, o_ref, acc_ref, carry_ref, *, tile):
    s_len = q_ref.shape[0]
    nt = s_len // tile
    row = lax.broadcasted_iota(jnp.int32, (tile, tile), 0)
    col = lax.broadcasted_iota(jnp.int32, (tile, tile), 1)
    causal = col < row
    neg_later = jnp.where(row > col, -1.0, 0.0).astype(BF16)
    for j in reversed(range(nt)):
        first = j == nt - 1
        lo, hi = j * tile, (j + 1) * tile
        k = k_ref[lo:hi, :]
        v = v_ref[lo:hi, :]
        z = lax.dot_general(q_ref[lo:, :], k, (((1,), (1,)), ((), ())), preferred_element_type=F32)
        sp = jnp.maximum(z, 0.0) + jnp.log2(1.0 + jnp.exp2(-jnp.abs(z)))
        sp_diag = jnp.where(causal, sp[:tile], 0.0)
        sp = sp_diag if first else jnp.concatenate([sp_diag, sp[tile:]], axis=0)
        suffix = jnp.dot(sp.astype(BF16), neg_later, preferred_element_type=F32)
        arg = (z - sp) + suffix
        if not first:
            carry = carry_ref[hi:, :]
            carry = jnp.concatenate([carry] * (tile // LANES), axis=1)
            arg = jnp.concatenate([arg[:tile], arg[tile:] + carry], axis=0)
        w = jnp.exp2(arg)
        w_diag = jnp.where(causal, w[:tile], 0.0)
        w = w_diag if first else jnp.concatenate([w_diag, w[tile:]], axis=0)
        pv = jnp.dot(w.astype(BF16), v, preferred_element_type=F32)
        total = -jnp.broadcast_to(jnp.sum(sp, axis=1, keepdims=True), (sp.shape[0], LANES))
        acc_ref[lo:hi, :] = pv[:tile]
        carry_ref[lo:hi, :] = total[:tile]
        if not first:
            acc_ref[hi:, :] += pv[tile:]
            carry_ref[hi:, :] += total[tile:]
    o_ref[...] = (acc_ref[...] * g_ref[...].astype(F32)).astype(o_ref.dtype)


def _attention(q, k, v, gate, *, batch, seq):
    m, d = q.shape
    heads = d // HEAD_DIM
    tile = min(256, seq)
    r3 = lambda a: a.reshape(batch, seq, d)
    spec = pl.BlockSpec((None, seq, HEAD_DIM), lambda b, h: (b, 0, h))
    est = 10 * seq * HEAD_DIM * 2 + 2 * seq * LANES * 4 + 8 * seq * tile * 4
    out = pl.pallas_call(
        functools.partial(_attn_kernel, tile=tile),
        out_shape=jax.ShapeDtypeStruct((batch, seq, d), BF16),
        grid=(batch, heads),
        in_specs=[spec, spec, spec, spec],
        out_specs=spec,
        scratch_shapes=[pltpu.VMEM((seq, HEAD_DIM), F32),
                        pltpu.VMEM((seq, LANES), F32)],
        compiler_params=pltpu.CompilerParams(
            dimension_semantics=("parallel", "parallel"),
            vmem_limit_bytes=_vmem_limit(est + (8 << 20))),
        name="stickbreak_attn",
    )(r3(q), r3(k), r3(v), r3(gate))
    return out.reshape(m, d)


def _ssm_prep_kernel(p_ref, bre_ref, bim_ref, cre_ref, cim_ref, t_ref, w_ref, vt_ref, lam_ref):
    n_rows = 24
    n = lax.broadcasted_iota(jnp.int32, (n_rows, LANES), 0).astype(F32)
    lane = lax.broadcasted_iota(jnp.int32, (GROUP, LANES), 1)

    def per_group(gi, _):
        a_re = p_ref[gi, 0:1, :]
        a_im = p_ref[gi, 1:2, :]
        dt = jnp.exp(p_ref[gi, 2:3, :])
        half_mask = p_ref[gi, 3:4, :]
        mag = jnp.exp(n * (a_re * dt))
        ang = n * (a_im * dt)
        pre = mag * jnp.cos(ang)
        pim = mag * jnp.sin(ang)
        lam_re = pre[1:2, :]
        lam_im = pim[1:2, :]
        den = a_re * a_re + a_im * a_im
        f_re = ((lam_re - 1.0) * a_re + lam_im * a_im) / den
        f_im = (lam_im * a_re - (lam_re - 1.0) * a_im) / den
        fre = f_re * pre - f_im * pim
        fim = f_re * pim + f_im * pre
        bre = bre_ref[gi]
        bim = bim_ref[gi]
        cre = cre_ref[gi]
        cim = cim_ref[gi]

        v_blocks = []
        for e in range(CHUNK + 1):
            va = cre * pre[e:e + 1, :] - cim * pim[e:e + 1, :]
            vb = -(cre * pim[e:e + 1, :] + cim * pre[e:e + 1, :])
            v_blocks.append(jnp.concatenate([va, vb], axis=1))
        for t in range(CHUNK):
            vt_ref[gi, t * GROUP:(t + 1) * GROUP, :] = v_blocks[t + 1].astype(vt_ref.dtype)
        v0t = jnp.concatenate(v_blocks[:CHUNK], axis=0)

        for s in range(CHUNK):
            e = CHUNK - 1 - s
            wa = fre[e:e + 1, :] * bre - fim[e:e + 1, :] * bim
            wb = fre[e:e + 1, :] * bim + fim[e:e + 1, :] * bre
            w_ref[gi, s * GROUP:(s + 1) * GROUP, :] = jnp.concatenate([wa, wb], axis=1).astype(w_ref.dtype)

        wb0 = jnp.concatenate([fre[0:1, :] * bre - fim[0:1, :] * bim,
                               fre[0:1, :] * bim + fim[0:1, :] * bre], axis=1)
        kmat = lax.dot_general(wb0, v0t, (((1,), (1,)), ((), ())),
                               precision=lax.Precision.HIGHEST, preferred_element_type=F32)
        k0 = kmat[:, :LANES]
        k1 = kmat[:, LANES:]
        zero = jnp.zeros_like(k0)
        for s in range(CHUNK):
            sh = s * GROUP
            if s == 0:
                lo, hi = k0, k1
            elif sh < LANES:
                r0 = pltpu.roll(k0, sh, 1)
                r1 = pltpu.roll(k1, sh, 1)
                lo = jnp.where(lane >= sh, r0, 0.0)
                hi = jnp.where(lane >= sh, r1, r0)
            elif sh == LANES:
                lo, hi = zero, k0
            else:
                r0 = pltpu.roll(k0, sh - LANES, 1)
                lo = zero
                hi = jnp.where(lane >= sh - LANES, r0, 0.0)
            t_ref[gi, s * GROUP:(s + 1) * GROUP, :] = jnp.concatenate([lo, hi], axis=1).astype(t_ref.dtype)

        lam_rows = jnp.concatenate([pre[CHUNK:CHUNK + 1, :] * half_mask, pim[CHUNK:CHUNK + 1, :] * half_mask,
                                    jnp.zeros((6, LANES), F32)], axis=0)
        lam_ref[gi] = lam_rows
        return 0

    lax.fori_loop(0, p_ref.shape[0], per_group, 0, unroll=True)


def _ssm_prep(a_re, a_im, log_dt, b_re, b_im, c_re, c_im):
    g = a_re.shape[0]
    even = (jnp.arange(g) % 2 == 0)[:, None, None]

    def pad_half(a):
        z = jnp.zeros_like(a)
        return jnp.where(even, jnp.concatenate([a, z], -1), jnp.concatenate([z, a], -1))

    dup = lambda a: jnp.concatenate([a, a], -1)
    half_mask = jnp.where(even[:, 0], jnp.concatenate([jnp.ones((g, STATE)), jnp.zeros((g, STATE))], -1),
                          jnp.concatenate([jnp.zeros((g, STATE)), jnp.ones((g, STATE))], -1))
    params = jnp.stack([dup(a_re), dup(a_im), jnp.broadcast_to(log_dt[:, None], (g, LANES)), half_mask]
                       + [jnp.zeros((g, LANES), F32)] * 4, axis=1).astype(F32)
    bt_re = pad_half(jnp.swapaxes(b_re, 1, 2))
    bt_im = pad_half(jnp.swapaxes(b_im, 1, 2))
    cp_re = pad_half(c_re)
    cp_im = pad_half(c_im)
    gb = GROUPS_PER_TILE
    kk = CHUNK * GROUP
    spec8 = pl.BlockSpec((gb, 8, LANES), lambda i: (i, 0, 0))
    spec16 = pl.BlockSpec((gb, GROUP, LANES), lambda i: (i, 0, 0))
    specm = pl.BlockSpec((gb, kk, kk), lambda i: (i, 0, 0))
    return pl.pallas_call(
        _ssm_prep_kernel,
        out_shape=(jax.ShapeDtypeStruct((g, kk, kk), BF16),
                   jax.ShapeDtypeStruct((g, kk, 2 * LANES), BF16),
                   jax.ShapeDtypeStruct((g, kk, 2 * LANES), BF16),
                   jax.ShapeDtypeStruct((g, 8, LANES), F32)),
        grid=(g // gb,),
        in_specs=[spec8, spec16, spec16, spec16, spec16],
        out_specs=(specm, specm, specm, spec8),
        compiler_params=pltpu.CompilerParams(dimension_semantics=("parallel",)),
        name="ssm_prep",
    )(params, bt_re, bt_im, cp_re, cp_im)


def _ssm_kernel(u_ref, t_ref, w_ref, vt_ref, lam_ref, d_ref, o_ref, zs, ys, ss, hs, yscr, *, batch, nchunk,
                block_rows):
    n_rows = batch * nchunk
    pairs = GROUPS_PER_TILE // 2
    pitch = nchunk + SCAN_PITCH_PAD
    lane_group = lax.broadcasted_iota(jnp.int32, (block_rows, LANES), 1) // GROUP
    bit_set = {d: (lane_group & d) != 0 for d in (4, 2, 1)}

    def transpose_pieces(xs):
        for d in (4, 2, 1):
            new = list(xs)
            for m in range(GROUPS_PER_TILE):
                if m & d:
                    continue
                a, b = xs[m], xs[m + d]
                new[m] = jnp.where(bit_set[d], pltpu.roll(b, GROUP * d, 1), a)
                new[m + d] = jnp.where(bit_set[d], b, pltpu.roll(a, LANES - GROUP * d, 1))
            xs = new
        return xs

    def position_rows(r0, t):
        return pl.ds(r0 * CHUNK + t, block_rows, stride=CHUNK)

    row_starts = range(0, n_rows, block_rows)

    for r0 in row_starts:
        for half in range(2):
            pieces = transpose_pieces([u_ref[position_rows(r0, half * 8 + m), :] for m in range(GROUPS_PER_TILE)])
            for g in range(GROUPS_PER_TILE):
                zs[g, r0:r0 + block_rows, half * LANES:(half + 1) * LANES] = pieces[g].astype(zs.dtype)

    for p in range(pairs):
        s_end = (jnp.dot(zs[2 * p], w_ref[2 * p], preferred_element_type=F32)
                 + jnp.dot(zs[2 * p + 1], w_ref[2 * p + 1], preferred_element_type=F32))
        for b in range(batch):
            ss[2 * p, b * pitch:b * pitch + nchunk, :] = s_end[b * nchunk:(b + 1) * nchunk, :LANES]
            ss[2 * p + 1, b * pitch:b * pitch + nchunk, :] = s_end[b * nchunk:(b + 1) * nchunk, LANES:]

    lre = [lam_ref[2 * p, 0:1, :] + lam_ref[2 * p + 1, 0:1, :] for p in range(pairs)]
    lim = [lam_ref[2 * p, 1:2, :] + lam_ref[2 * p + 1, 1:2, :] for p in range(pairs)]

    def scan(c, carry):
        new = []
        for p in range(pairs):
            h_re, h_im = carry[2 * p], carry[2 * p + 1]
            rows = pl.ds(c, batch, stride=pitch)
            hs[2 * p, rows, :] = h_re
            hs[2 * p + 1, rows, :] = h_im
            s_re = ss[2 * p, rows, :]
            s_im = ss[2 * p + 1, rows, :]
            new.append(h_re * lre[p] - h_im * lim[p] + s_re)
            new.append(h_im * lre[p] + h_re * lim[p] + s_im)
        return tuple(new)

    zero_state = jnp.zeros((batch, LANES), F32)
    lax.fori_loop(0, nchunk, scan, tuple(zero_state for _ in range(2 * pairs)))

    def entering(q):
        return jnp.concatenate([hs[q, b * pitch:b * pitch + nchunk, :] for b in range(batch)], axis=0)

    for g in range(GROUPS_PER_TILE):
        h_in = jnp.concatenate([entering(2 * (g // 2)), entering(2 * (g // 2) + 1)], axis=1).astype(BF16)
        ys[g] = (jnp.dot(zs[g], t_ref[g], preferred_element_type=F32)
                 + lax.dot_general(h_in, vt_ref[g], (((1,), (1,)), ((), ())), preferred_element_type=F32))

    d_row = d_ref[...]
    for r0 in row_starts:
        for half in range(2):
            pieces = transpose_pieces([ys[g, r0:r0 + block_rows, half * LANES:(half + 1) * LANES]
                                       for g in range(GROUPS_PER_TILE)])
            for m in range(GROUPS_PER_TILE):
                rows = position_rows(r0, half * 8 + m)
                yscr[rows, :] = jax.nn.gelu(pieces[m] + d_row * u_ref[rows, :])

    o_ref[...] = yscr[...].astype(o_ref.dtype)


def _ssm(u, tmat, wmat, vtmat, lam, d_vec, *, batch, seq):
    m, d = u.shape
    nchunk = seq // CHUNK
    n_rows = batch * nchunk
    n_pad = batch * (nchunk + SCAN_PITCH_PAD)
    gb = GROUPS_PER_TILE
    kk = CHUNK * GROUP
    block_rows = min(n_rows, 128)
    specm = pl.BlockSpec((gb, kk, kk), lambda i: (i, 0, 0))
    est = (2 * m * LANES * (4 + 2) + 2 * 3 * gb * kk * kk * 2 + gb * n_rows * kk * (2 + 4)
           + 2 * gb * n_pad * LANES * 4 + m * LANES * 4 + 8 * n_rows * kk * 4)
    return pl.pallas_call(
        functools.partial(_ssm_kernel, batch=batch, nchunk=nchunk, block_rows=block_rows),
        out_shape=jax.ShapeDtypeStruct((m, d), BF16),
        grid=(d // LANES,),
        in_specs=[pl.BlockSpec((m, LANES), lambda i: (0, i)),
                  specm, specm, specm,
                  pl.BlockSpec((gb, 8, LANES), lambda i: (i, 0, 0)),
                  pl.BlockSpec((1, LANES), lambda i: (0, i))],
        out_specs=pl.BlockSpec((m, LANES), lambda i: (0, i)),
        scratch_shapes=[pltpu.VMEM((gb, n_rows, kk), BF16),
                        pltpu.VMEM((gb, n_rows, kk), F32),
                        pltpu.VMEM((gb, n_pad, LANES), F32),
                        pltpu.VMEM((gb, n_pad, LANES), F32),
                        pltpu.VMEM((m, LANES), F32)],
        compiler_params=pltpu.CompilerParams(
            dimension_semantics=("parallel",),
            vmem_limit_bytes=_vmem_limit(est)),
        name="s5_scan",
    )(u, tmat, wmat, vtmat, lam, d_vec.reshape(1, d))


def kernel(x, norm_g, attn_w_in, attn_q_g, attn_k_g, attn_w_out, ssm_w_in, ssm_A_re, ssm_A_im, ssm_log_dt,
           ssm_B_re, ssm_B_im, ssm_C_re, ssm_C_im, ssm_D, ssm_glu_w, ssm_glu_b, ssm_w_out):
    batch, seq, d = x.shape
    m = batch * seq
    assert d % LANES == 0 and seq % 256 == 0 and (m // CHUNK) % 8 == 0
    assert ssm_A_re.shape[1:] == (d // GROUP, STATE) and (d // GROUP) % GROUPS_PER_TILE == 0
    x2 = x.reshape(m, d)

    h = _rmsnorm(x2, norm_g[0])
    w_in = attn_w_in[0]
    qg = attn_q_g[0].reshape(1, HEAD_DIM)
    kg = attn_k_g[0].reshape(1, HEAD_DIM)
    q = _matmul(h, w_in, col_off=0, n=d, mode="headnorm", out_dtype=BF16, extras=(qg,),
                scale=LOG2E / math.sqrt(HEAD_DIM))
    k = _matmul(h, w_in, col_off=d, n=d, mode="headnorm", out_dtype=BF16, extras=(kg,))
    v = _matmul(h, w_in, col_off=2 * d, n=d, mode="store", out_dtype=BF16)
    gate = _matmul(h, w_in, col_off=3 * d, n=d, mode="silu", out_dtype=BF16)
    og = _attention(q, k, v, gate, batch=batch, seq=seq)
    x2 = _matmul(og, attn_w_out[0], col_off=0, n=d, mode="residual", out_dtype=F32, extras=(x2,))

    h = _rmsnorm(x2, norm_g[1])
    w_in = ssm_w_in[0]
    u = _matmul(h, w_in, col_off=0, n=d, mode="store", out_dtype=F32)
    gate = _matmul(h, w_in, col_off=d, n=d, mode="silu", out_dtype=BF16)
    tmat, wmat, vtmat, lam = _ssm_prep(ssm_A_re[0], ssm_A_im[0], ssm_log_dt[0], ssm_B_re[0], ssm_B_im[0],
                                       ssm_C_re[0], ssm_C_im[0])
    y = _ssm(u, tmat, wmat, vtmat, lam, ssm_D[0], batch=batch, seq=seq)
    yg = _matmul(y, ssm_glu_w[0], col_off=0, n=d, mode="glu", out_dtype=BF16,
                 extras=(ssm_glu_b[0].reshape(1, d), y, gate))
    x2 = _matmul(yg, ssm_w_out[0], col_off=0, n=d, mode="residual", out_dtype=F32, extras=(x2,))
    return x2.reshape(batch, seq, d)
```

```python
import functools
import math

import jax
import jax.numpy as jnp
from jax import lax
from jax.experimental import pallas as pl
from jax.experimental.pallas import tpu as pltpu

HEAD_DIM = 128
GROUP = 16
STATE = 64
RMS_EPS = 1e-6
LANES = 128
CHUNK = 16
LOG2E = 1.4426950408889634
SCAN_PITCH_PAD = 8
GROUPS_PER_TILE = LANES // GROUP
VMEM_CAP_BYTES = 60 * 1024 * 1024

F32 = jnp.float32
BF16 = jnp.bfloat16


def _vmem_limit(estimate_bytes):
    return int(min(VMEM_CAP_BYTES, max(32 * 1024 * 1024, estimate_bytes)))


def _rmsnorm_kernel(x_ref, g_ref, o_ref):
    x = x_ref[...]
    r = lax.rsqrt(jnp.mean(x * x, axis=-1, keepdims=True) + RMS_EPS)
    o_ref[...] = ((x * r) * g_ref[...]).astype(o_ref.dtype)


def _rmsnorm(x2, g):
    m, d = x2.shape
    tm = min(512, m)
    return pl.pallas_call(
        _rmsnorm_kernel,
        out_shape=jax.ShapeDtypeStruct((m, d), BF16),
        grid=(m // tm,),
        in_specs=[pl.BlockSpec((tm, d), lambda i: (i, 0)),
                  pl.BlockSpec((1, d), lambda i: (0, 0))],
        out_specs=pl.BlockSpec((tm, d), lambda i: (i, 0)),
        compiler_params=pltpu.CompilerParams(dimension_semantics=("parallel",),
                                             vmem_limit_bytes=_vmem_limit(2 * tm * d * (4 + 2) + 4 * tm * d * 4)),
        name="rmsnorm",
    )(x2, g.reshape(1, d))


def _mm_kernel(x_ref, w_ref, *rest, mode, scale):
    acc = jnp.dot(x_ref[...], w_ref[...].astype(BF16), preferred_element_type=F32)
    o_ref = rest[-1]
    if mode == "headnorm":
        g = rest[0][...]
        for h in range(acc.shape[1] // HEAD_DIM):
            sl = slice(h * HEAD_DIM, (h + 1) * HEAD_DIM)
            a = acc[:, sl]
            r = lax.rsqrt(jnp.mean(a * a, axis=-1, keepdims=True) + RMS_EPS)
            o_ref[:, sl] = (((a * r) * g) * scale).astype(o_ref.dtype)
    elif mode == "silu":
        o_ref[...] = (acc * jax.nn.sigmoid(acc)).astype(o_ref.dtype)
    elif mode == "residual":
        o_ref[...] = (rest[0][...] + acc).astype(o_ref.dtype)
    elif mode == "glu":
        b_ref, y_ref, gate_ref = rest[0], rest[1], rest[2]
        t = acc + b_ref[...]
        o_ref[...] = (y_ref[...].astype(F32) * jax.nn.sigmoid(t) * gate_ref[...].astype(F32)).astype(o_ref.dtype)
    else:
        o_ref[...] = acc.astype(o_ref.dtype)


def _matmul(x, w, *, col_off, n, mode, out_dtype, extras=(), scale=1.0, tm=512, tn=1024):
    m, k = x.shape
    tm = min(tm, m)
    tn = min(tn, n)
    off = col_off // tn
    in_specs = [pl.BlockSpec((tm, k), lambda j, i: (i, 0)),
                pl.BlockSpec((k, tn), lambda j, i: (0, j + off))]
    extra_bytes = 0
    for e in extras:
        if e.shape[0] == 1:
            width = e.shape[1]
            if width == n:
                in_specs.append(pl.BlockSpec((1, tn), lambda j, i: (0, j)))
            else:
                in_specs.append(pl.BlockSpec((1, width), lambda j, i: (0, 0)))
        else:
            in_specs.append(pl.BlockSpec((tm, tn), lambda j, i: (i, j)))
            extra_bytes += tm * tn * e.dtype.itemsize
    out_bytes = tm * tn * jnp.dtype(out_dtype).itemsize
    est = (2 * (tm * k * 2 + k * tn * w.dtype.itemsize + out_bytes + extra_bytes)
           + k * tn * 2 + 3 * tm * tn * 4)
    return pl.pallas_call(
        functools.partial(_mm_kernel, mode=mode, scale=scale),
        out_shape=jax.ShapeDtypeStruct((m, n), out_dtype),
        grid=(n // tn, m // tm),
        in_specs=in_specs,
        out_specs=pl.BlockSpec((tm, tn), lambda j, i: (i, j)),
        compiler_params=pltpu.CompilerParams(
            dimension_semantics=("parallel", "parallel"),
            vmem_limit_bytes=_vmem_limit(est + (4 << 20))),
        name="mm_" + mode,
    )(x, w, *extras)


def _attn_kernel(q_ref, k_ref, v_ref, g_ref, o_ref, acc_ref, carry_ref, *, tile, heads):
    s_len = q_ref.shape[0]
    nt = s_len // tile
    row = lax.broadcasted_iota(jnp.int32, (tile, tile), 0)
    col = lax.broadcasted_iota(jnp.int32, (tile, tile), 1)
    causal = col < row
    neg_later = jnp.where(row > col, -1.0, 0.0).astype(BF16)
    for j in reversed(range(nt)):
        first = j == nt - 1
        lo, hi = j * tile, (j + 1) * tile
        for h in range(heads):
            cols = slice(h * HEAD_DIM, (h + 1) * HEAD_DIM)
            k = k_ref[lo:hi, cols]
            v = v_ref[lo:hi, cols]
            z = lax.dot_general(q_ref[lo:, cols], k, (((1,), (1,)), ((), ())), preferred_element_type=F32)
            sp = jnp.maximum(z, 0.0) + jnp.log2(1.0 + jnp.exp2(-jnp.abs(z)))
            sp_diag = jnp.where(causal, sp[:tile], 0.0)
            sp = sp_diag if first else jnp.concatenate([sp_diag, sp[tile:]], axis=0)
            suffix = jnp.dot(sp.astype(BF16), neg_later, preferred_element_type=F32)
            arg = (z - sp) + suffix
            if not first:
                carry = carry_ref[hi:, cols]
                carry = jnp.concatenate([carry] * (tile // LANES), axis=1)
                arg = jnp.concatenate([arg[:tile], arg[tile:] + carry], axis=0)
            w = jnp.exp2(arg)
            w_diag = jnp.where(causal, w[:tile], 0.0)
            w = w_diag if first else jnp.concatenate([w_diag, w[tile:]], axis=0)
            pv = jnp.dot(w.astype(BF16), v, preferred_element_type=F32)
            total = -jnp.broadcast_to(jnp.sum(sp, axis=1, keepdims=True), (sp.shape[0], LANES))
            acc_ref[lo:hi, cols] = pv[:tile]
            carry_ref[lo:hi, cols] = total[:tile]
            if not first:
                acc_ref[hi:, cols] += pv[tile:]
                carry_ref[hi:, cols] += total[tile:]
    o_ref[...] = (acc_ref[...] * g_ref[...].astype(F32)).astype(o_ref.dtype)


def _attention(q, k, v, gate, *, batch, seq):
    m, d = q.shape
    heads = 2 if d % (2 * HEAD_DIM) == 0 else 1
    width = heads * HEAD_DIM
    tile = min(256, seq)
    r3 = lambda a: a.reshape(batch, seq, d)
    spec = pl.BlockSpec((None, seq, width), lambda b, h: (b, 0, h))
    est = 10 * seq * width * 2 + 2 * seq * width * 4 + 8 * heads * seq * tile * 4
    out = pl.pallas_call(
        functools.partial(_attn_kernel, tile=tile, heads=heads),
        out_shape=jax.ShapeDtypeStruct((batch, seq, d), BF16),
        grid=(batch, d // width),
        in_specs=[spec, spec, spec, spec],
        out_specs=spec,
        scratch_shapes=[pltpu.VMEM((seq, width), F32),
                        pltpu.VMEM((seq, width), F32)],
        compiler_params=pltpu.CompilerParams(
            dimension_semantics=("parallel", "parallel"),
            vmem_limit_bytes=_vmem_limit(est + (8 << 20))),
        name="stickbreak_attn",
    )(r3(q), r3(k), r3(v), r3(gate))
    return out.reshape(m, d)


def _ssm_prep_kernel(p_ref, bre_ref, bim_ref, cre_ref, cim_ref, t_ref, w_ref, vt_ref, lam_ref):
    n_rows = 24
    n = lax.broadcasted_iota(jnp.int32, (n_rows, LANES), 0).astype(F32)
    lane = lax.broadcasted_iota(jnp.int32, (GROUP, LANES), 1)

    def per_group(gi, _):
        a_re = p_ref[gi, 0:1, :]
        a_im = p_ref[gi, 1:2, :]
        dt = jnp.exp(p_ref[gi, 2:3, :])
        half_mask = p_ref[gi, 3:4, :]
        mag = jnp.exp(n * (a_re * dt))
        ang = n * (a_im * dt)
        pre = mag * jnp.cos(ang)
        pim = mag * jnp.sin(ang)
        lam_re = pre[1:2, :]
        lam_im = pim[1:2, :]
        den = a_re * a_re + a_im * a_im
        f_re = ((lam_re - 1.0) * a_re + lam_im * a_im) / den
        f_im = (lam_im * a_re - (lam_re - 1.0) * a_im) / den
        fre = f_re * pre - f_im * pim
        fim = f_re * pim + f_im * pre
        bre = bre_ref[gi]
        bim = bim_ref[gi]
        cre = cre_ref[gi]
        cim = cim_ref[gi]

        v_blocks = []
        for e in range(CHUNK + 1):
            va = cre * pre[e:e + 1, :] - cim * pim[e:e + 1, :]
            vb = -(cre * pim[e:e + 1, :] + cim * pre[e:e + 1, :])
            v_blocks.append(jnp.concatenate([va, vb], axis=1))
        for t in range(CHUNK):
            vt_ref[gi, t * GROUP:(t + 1) * GROUP, :] = v_blocks[t + 1].astype(vt_ref.dtype)
        v0t = jnp.concatenate(v_blocks[:CHUNK], axis=0)

        for s in range(CHUNK):
            e = CHUNK - 1 - s
            wa = fre[e:e + 1, :] * bre - fim[e:e + 1, :] * bim
            wb = fre[e:e + 1, :] * bim + fim[e:e + 1, :] * bre
            w_ref[gi, s * GROUP:(s + 1) * GROUP, :] = jnp.concatenate([wa, wb], axis=1).astype(w_ref.dtype)

        wb0 = jnp.concatenate([fre[0:1, :] * bre - fim[0:1, :] * bim,
                               fre[0:1, :] * bim + fim[0:1, :] * bre], axis=1)
        kmat = lax.dot_general(wb0, v0t, (((1,), (1,)), ((), ())),
                               precision=lax.Precision.HIGHEST, preferred_element_type=F32)
        k0 = kmat[:, :LANES]
        k1 = kmat[:, LANES:]
        zero = jnp.zeros_like(k0)
        for s in range(CHUNK):
            sh = s * GROUP
            if s == 0:
                lo, hi = k0, k1
            elif sh < LANES:
                r0 = pltpu.roll(k0, sh, 1)
                r1 = pltpu.roll(k1, sh, 1)
                lo = jnp.where(lane >= sh, r0, 0.0)
                hi = jnp.where(lane >= sh, r1, r0)
            elif sh == LANES:
                lo, hi = zero, k0
            else:
                r0 = pltpu.roll(k0, sh - LANES, 1)
                lo = zero
                hi = jnp.where(lane >= sh - LANES, r0, 0.0)
            t_ref[gi, s * GROUP:(s + 1) * GROUP, :] = jnp.concatenate([lo, hi], axis=1).astype(t_ref.dtype)

        lam_rows = jnp.concatenate([pre[CHUNK:CHUNK + 1, :] * half_mask, pim[CHUNK:CHUNK + 1, :] * half_mask,
                                    jnp.zeros((6, LANES), F32)], axis=0)
        lam_ref[gi] = lam_rows
        return 0

    lax.fori_loop(0, p_ref.shape[0], per_group, 0, unroll=True)


def _ssm_prep(a_re, a_im, log_dt, b_re, b_im, c_re, c_im):
    g = a_re.shape[0]
    even = (jnp.arange(g) % 2 == 0)[:, None, None]

    def pad_half(a):
        z = jnp.zeros_like(a)
        return jnp.where(even, jnp.concatenate([a, z], -1), jnp.concatenate([z, a], -1))

    dup = lambda a: jnp.concatenate([a, a], -1)
    half_mask = jnp.where(even[:, 0], jnp.concatenate([jnp.ones((g, STATE)), jnp.zeros((g, STATE))], -1),
                          jnp.concatenate([jnp.zeros((g, STATE)), jnp.ones((g, STATE))], -1))
    params = jnp.stack([dup(a_re), dup(a_im), jnp.broadcast_to(log_dt[:, None], (g, LANES)), half_mask]
                       + [jnp.zeros((g, LANES), F32)] * 4, axis=1).astype(F32)
    bt_re = pad_half(jnp.swapaxes(b_re, 1, 2))
    bt_im = pad_half(jnp.swapaxes(b_im, 1, 2))
    cp_re = pad_half(c_re)
    cp_im = pad_half(c_im)
    gb = GROUPS_PER_TILE
    kk = CHUNK * GROUP
    spec8 = pl.BlockSpec((gb, 8, LANES), lambda i: (i, 0, 0))
    spec16 = pl.BlockSpec((gb, GROUP, LANES), lambda i: (i, 0, 0))
    specm = pl.BlockSpec((gb, kk, kk), lambda i: (i, 0, 0))
    return pl.pallas_call(
        _ssm_prep_kernel,
        out_shape=(jax.ShapeDtypeStruct((g, kk, kk), BF16),
                   jax.ShapeDtypeStruct((g, kk, 2 * LANES), BF16),
                   jax.ShapeDtypeStruct((g, kk, 2 * LANES), BF16),
                   jax.ShapeDtypeStruct((g, 8, LANES), F32)),
        grid=(g // gb,),
        in_specs=[spec8, spec16, spec16, spec16, spec16],
        out_specs=(specm, specm, specm, spec8),
        compiler_params=pltpu.CompilerParams(dimension_semantics=("parallel",)),
        name="ssm_prep",
    )(params, bt_re, bt_im, cp_re, cp_im)


def _ssm_kernel(u_ref, t_ref, w_ref, vt_ref, lam_ref, d_ref, o_ref, zs, ys, ss, hs, yscr, *, batch, nchunk,
                block_rows):
    n_rows = batch * nchunk
    pairs = GROUPS_PER_TILE // 2
    pitch = nchunk + SCAN_PITCH_PAD
    lane_group = lax.broadcasted_iota(jnp.int32, (block_rows, LANES), 1) // GROUP
    bit_set = {d: (lane_group & d) != 0 for d in (4, 2, 1)}

    def transpose_pieces(xs):
        for d in (4, 2, 1):
            new = list(xs)
            for m in range(GROUPS_PER_TILE):
                if m & d:
                    continue
                a, b = xs[m], xs[m + d]
                new[m] = jnp.where(bit_set[d], pltpu.roll(b, GROUP * d, 1), a)
                new[m + d] = jnp.where(bit_set[d], b, pltpu.roll(a, LANES - GROUP * d, 1))
            xs = new
        return xs

    def position_rows(r0, t):
        return pl.ds(r0 * CHUNK + t, block_rows, stride=CHUNK)

    row_starts = range(0, n_rows, block_rows)

    for r0 in row_starts:
        for half in range(2):
            pieces = transpose_pieces([u_ref[position_rows(r0, half * 8 + m), :] for m in range(GROUPS_PER_TILE)])
            for g in range(GROUPS_PER_TILE):
                zs[g, r0:r0 + block_rows, half * LANES:(half + 1) * LANES] = pieces[g].astype(zs.dtype)

    for p in range(pairs):
        s_end = (jnp.dot(zs[2 * p], w_ref[2 * p], preferred_element_type=F32)
                 + jnp.dot(zs[2 * p + 1], w_ref[2 * p + 1], preferred_element_type=F32))
        for b in range(batch):
            ss[2 * p, b * pitch:b * pitch + nchunk, :] = s_end[b * nchunk:(b + 1) * nchunk, :LANES]
            ss[2 * p + 1, b * pitch:b * pitch + nchunk, :] = s_end[b * nchunk:(b + 1) * nchunk, LANES:]

    lre = [lam_ref[2 * p, 0:1, :] + lam_ref[2 * p + 1, 0:1, :] for p in range(pairs)]
    lim = [lam_ref[2 * p, 1:2, :] + lam_ref[2 * p + 1, 1:2, :] for p in range(pairs)]

    def scan(c, carry):
        new = []
        for p in range(pairs):
            h_re, h_im = carry[2 * p], carry[2 * p + 1]
            rows = pl.ds(c, batch, stride=pitch)
            hs[2 * p, rows, :] = h_re
            hs[2 * p + 1, rows, :] = h_im
            s_re = ss[2 * p, rows, :]
            s_im = ss[2 * p + 1, rows, :]
            new.append(h_re * lre[p] - h_im * lim[p] + s_re)
            new.append(h_im * lre[p] + h_re * lim[p] + s_im)
        return tuple(new)

    zero_state = jnp.zeros((batch, LANES), F32)
    lax.fori_loop(0, nchunk, scan, tuple(zero_state for _ in range(2 * pairs)))

    def entering(q):
        return jnp.concatenate([hs[q, b * pitch:b * pitch + nchunk, :] for b in range(batch)], axis=0)

    for g in range(GROUPS_PER_TILE):
        h_in = jnp.concatenate([entering(2 * (g // 2)), entering(2 * (g // 2) + 1)], axis=1).astype(BF16)
        ys[g] = (jnp.dot(zs[g], t_ref[g], preferred_element_type=F32)
                 + lax.dot_general(h_in, vt_ref[g], (((1,), (1,)), ((), ())), preferred_element_type=F32))

    d_row = d_ref[...]
    for r0 in row_starts:
        for half in range(2):
            pieces = transpose_pieces([ys[g, r0:r0 + block_rows, half * LANES:(half + 1) * LANES]
                                       for g in range(GROUPS_PER_TILE)])
            for m in range(GROUPS_PER_TILE):
                rows = position_rows(r0, half * 8 + m)
                yscr[rows, :] = jax.nn.gelu(pieces[m] + d_row * u_ref[rows, :])

    o_ref[...] = yscr[...].astype(o_ref.dtype)


def _ssm(u, tmat, wmat, vtmat, lam, d_vec, *, batch, seq):
    m, d = u.shape
    nchunk = seq // CHUNK
    n_rows = batch * nchunk
    n_pad = batch * (nchunk + SCAN_PITCH_PAD)
    gb = GROUPS_PER_TILE
    kk = CHUNK * GROUP
    block_rows = min(n_rows, 128)
    specm = pl.BlockSpec((gb, kk, kk), lambda i: (i, 0, 0))
    est = (2 * m * LANES * (4 + 2) + 2 * 3 * gb * kk * kk * 2 + gb * n_rows * kk * (2 + 4)
           + 2 * gb * n_pad * LANES * 4 + m * LANES * 4 + 8 * n_rows * kk * 4)
    return pl.pallas_call(
        functools.partial(_ssm_kernel, batch=batch, nchunk=nchunk, block_rows=block_rows),
        out_shape=jax.ShapeDtypeStruct((m, d), BF16),
        grid=(d // LANES,),
        in_specs=[pl.BlockSpec((m, LANES), lambda i: (0, i)),
                  specm, specm, specm,
                  pl.BlockSpec((gb, 8, LANES), lambda i: (i, 0, 0)),
                  pl.BlockSpec((1, LANES), lambda i: (0, i))],
        out_specs=pl.BlockSpec((m, LANES), lambda i: (0, i)),
        scratch_shapes=[pltpu.VMEM((gb, n_rows, kk), BF16),
                        pltpu.VMEM((gb, n_rows, kk), F32),
                        pltpu.VMEM((gb, n_pad, LANES), F32),
                        pltpu.VMEM((gb, n_pad, LANES), F32),
                        pltpu.VMEM((m, LANES), F32)],
        compiler_params=pltpu.CompilerParams(
            dimension_semantics=("parallel",),
            vmem_limit_bytes=_vmem_limit(est)),
        name="s5_scan",
    )(u, tmat, wmat, vtmat, lam, d_vec.reshape(1, d))


def kernel(x, norm_g, attn_w_in, attn_q_g, attn_k_g, attn_w_out, ssm_w_in, ssm_A_re, ssm_A_im, ssm_log_dt,
           ssm_B_re, ssm_B_im, ssm_C_re, ssm_C_im, ssm_D, ssm_glu_w, ssm_glu_b, ssm_w_out):
    batch, seq, d = x.shape
    m = batch * seq
    assert d % LANES == 0 and seq % 256 == 0 and (m // CHUNK) % 8 == 0
    assert ssm_A_re.shape[1:] == (d // GROUP, STATE) and (d // GROUP) % GROUPS_PER_TILE == 0
    x2 = x.reshape(m, d)

    h = _rmsnorm(x2, norm_g[0])
    w_in = attn_w_in[0]
    qg = attn_q_g[0].reshape(1, HEAD_DIM)
    kg = attn_k_g[0].reshape(1, HEAD_DIM)
    q = _matmul(h, w_in, col_off=0, n=d, mode="headnorm", out_dtype=BF16, extras=(qg,),
                scale=LOG2E / math.sqrt(HEAD_DIM))
    k = _matmul(h, w_in, col_off=d, n=d, mode="headnorm", out_dtype=BF16, extras=(kg,))
    v = _matmul(h, w_in, col_off=2 * d, n=d, mode="store", out_dtype=BF16)
    gate = _matmul(h, w_in, col_off=3 * d, n=d, mode="silu", out_dtype=BF16)
    og = _attention(q, k, v, gate, batch=batch, seq=seq)
    x2 = _matmul(og, attn_w_out[0], col_off=0, n=d, mode="residual", out_dtype=F32, extras=(x2,))

    h = _rmsnorm(x2, norm_g[1])
    w_in = ssm_w_in[0]
    u = _matmul(h, w_in, col_off=0, n=d, mode="store", out_dtype=F32)
    gate = _matmul(h, w_in, col_off=d, n=d, mode="silu", out_dtype=BF16)
    tmat, wmat, vtmat, lam = _ssm_prep(ssm_A_re[0], ssm_A_im[0], ssm_log_dt[0], ssm_B_re[0], ssm_B_im[0],
                                       ssm_C_re[0], ssm_C_im[0])
    y = _ssm(u, tmat, wmat, vtmat, lam, ssm_D[0], batch=batch, seq=seq)
    yg = _matmul(y, ssm_glu_w[0], col_off=0, n=d, mode="glu", out_dtype=BF16,
                 extras=(ssm_glu_b[0].reshape(1, d), y, gate))
    x2 = _matmul(yg, ssm_w_out[0], col_off=0, n=d, mode="residual", out_dtype=F32, extras=(x2,))
    return x2.reshape(batch, seq, d)
```

```python
import functools
import math

import jax
import jax.numpy as jnp
from jax import lax
from jax.experimental import pallas as pl
from jax.experimental.pallas import tpu as pltpu

HEAD_DIM = 128
GROUP = 16
STATE = 64
RMS_EPS = 1e-6
LANES = 128
CHUNK = 16
LOG2E = 1.4426950408889634
SCAN_PITCH_PAD = 8
GROUPS_PER_TILE = LANES // GROUP
VMEM_CAP_BYTES = 60 * 1024 * 1024

F32 = jnp.float32
BF16 = jnp.bfloat16


def _vmem_limit(estimate_bytes):
    return int(min(VMEM_CAP_BYTES, max(32 * 1024 * 1024, estimate_bytes)))


def _rmsnorm_kernel(x_ref, g_ref, o_ref):
    x = x_ref[...]
    r = lax.rsqrt(jnp.mean(x * x, axis=-1, keepdims=True) + RMS_EPS)
    o_ref[...] = ((x * r) * g_ref[...]).astype(o_ref.dtype)


def _rmsnorm(x2, g):
    m, d = x2.shape
    tm = min(512, m)
    return pl.pallas_call(
        _rmsnorm_kernel,
        out_shape=jax.ShapeDtypeStruct((m, d), BF16),
        grid=(m // tm,),
        in_specs=[pl.BlockSpec((tm, d), lambda i: (i, 0)),
                  pl.BlockSpec((1, d), lambda i: (0, 0))],
        out_specs=pl.BlockSpec((tm, d), lambda i: (i, 0)),
        compiler_params=pltpu.CompilerParams(dimension_semantics=("parallel",),
                                             vmem_limit_bytes=_vmem_limit(2 * tm * d * (4 + 2) + 4 * tm * d * 4)),
        name="rmsnorm",
    )(x2, g.reshape(1, d))


def _mm_epilogue(acc, extras, outs, *, mode, scale, row_ssq, emit_norm):
    extras = list(extras)
    if row_ssq:
        ssq = extras.pop(0)[...]
        total = sum(ssq[:, c * LANES:(c + 1) * LANES] for c in range(ssq.shape[1] // LANES))
        r = lax.rsqrt(total * (1.0 / row_ssq) + RMS_EPS)
        acc = acc * jnp.concatenate([r] * (acc.shape[1] // LANES), axis=1)
    o_ref = outs[0]
    if mode == "headnorm":
        g = extras[0][...]
        for h in range(acc.shape[1] // HEAD_DIM):
            sl = slice(h * HEAD_DIM, (h + 1) * HEAD_DIM)
            a = acc[:, sl]
            r = lax.rsqrt(jnp.mean(a * a, axis=-1, keepdims=True) + RMS_EPS)
            o_ref[:, sl] = (((a * r) * g) * scale).astype(o_ref.dtype)
    elif mode == "silu":
        o_ref[...] = (acc * jax.nn.sigmoid(acc)).astype(o_ref.dtype)
    elif mode == "residual":
        y = extras[0][...] + acc
        o_ref[...] = y.astype(o_ref.dtype)
        if emit_norm:
            outs[1][...] = (y * extras[1][...]).astype(outs[1].dtype)
            outs[2][...] = jnp.broadcast_to(jnp.sum(y * y, axis=1, keepdims=True), outs[2].shape)
    elif mode == "glu":
        b_ref, y_ref, gate_ref = extras[0], extras[1], extras[2]
        t = acc + b_ref[...]
        o_ref[...] = (y_ref[...].astype(F32) * jax.nn.sigmoid(t) * gate_ref[...].astype(F32)).astype(o_ref.dtype)
    else:
        o_ref[...] = acc.astype(o_ref.dtype)


def _mm_kernel(x_ref, w_ref, *rest, mode, scale, row_tiles, n_out, row_ssq, emit_norm):
    acc_ref = rest[-1]
    outs = rest[-1 - n_out:-1]
    extras = rest[:-1 - n_out]
    i = pl.program_id(1)
    finish = functools.partial(_mm_epilogue, mode=mode, scale=scale, row_ssq=row_ssq, emit_norm=emit_norm)

    def product():
        return jnp.dot(x_ref[...], w_ref[...].astype(BF16), preferred_element_type=F32)

    @pl.when(i == 0)
    def _():
        acc_ref[...] = product()

    @pl.when(jnp.logical_and(i > 0, i < row_tiles))
    def _():
        previous = acc_ref[...]
        acc_ref[...] = product()
        finish(previous, extras, outs)

    @pl.when(i == row_tiles)
    def _():
        finish(acc_ref[...], extras, outs)


def _matmul(x, w, *, col_off, n, mode, out_dtype, extras=(), scale=1.0, tm=512, tn=1024, row_ssq=None,
            next_gain=None):
    m, k = x.shape
    tm = min(tm, m)
    tn = min(tn, n)
    off = col_off // tn
    row_tiles = m // tm
    col_tiles = n // tn
    behind = lambda i: jnp.maximum(i - 1, 0)
    tile_spec = pl.BlockSpec((tm, tn), lambda j, i: (behind(i), j))
    in_specs = [pl.BlockSpec((tm, k), lambda j, i: (jnp.minimum(i, row_tiles - 1), 0)),
                pl.BlockSpec((k, tn), lambda j, i: (0, j + off))]
    operands = [x, w]
    extra_bytes = 0
    if row_ssq is not None:
        operands.append(row_ssq)
        in_specs.append(pl.BlockSpec((tm, row_ssq.shape[1]), lambda j, i: (behind(i), 0)))
    extras = tuple(extras) + ((next_gain.reshape(1, n),) if next_gain is not None else ())
    for e in extras:
        operands.append(e)
        if e.shape[0] == 1:
            width = e.shape[1]
            if width == n:
                in_specs.append(pl.BlockSpec((1, tn), lambda j, i: (0, j)))
            else:
                in_specs.append(pl.BlockSpec((1, width), lambda j, i: (0, 0)))
        else:
            in_specs.append(tile_spec)
            extra_bytes += tm * tn * e.dtype.itemsize
    out_shape = [jax.ShapeDtypeStruct((m, n), out_dtype)]
    out_specs = [tile_spec]
    if next_gain is not None:
        out_shape += [jax.ShapeDtypeStruct((m, n), BF16), jax.ShapeDtypeStruct((m, col_tiles * LANES), F32)]
        out_specs += [tile_spec, pl.BlockSpec((tm, LANES), lambda j, i: (behind(i), j))]
        extra_bytes += tm * tn * 2
    out_bytes = tm * tn * jnp.dtype(out_dtype).itemsize
    est = (2 * (tm * k * 2 + k * tn * w.dtype.itemsize + out_bytes + extra_bytes)
           + k * tn * 2 + 4 * tm * tn * 4)
    outs = pl.pallas_call(
        functools.partial(_mm_kernel, mode=mode, scale=scale, row_tiles=row_tiles, n_out=len(out_shape),
                          row_ssq=(k if row_ssq is not None else None), emit_norm=next_gain is not None),
        out_shape=out_shape,
        grid=(col_tiles, row_tiles + 1),
        in_specs=in_specs,
        out_specs=out_specs,
        scratch_shapes=[pltpu.VMEM((tm, tn), F32)],
        compiler_params=pltpu.CompilerParams(
            dimension_semantics=("parallel", "arbitrary"),
            vmem_limit_bytes=_vmem_limit(est + (4 << 20))),
        name="mm_" + mode,
    )(*operands)
    return outs[0] if next_gain is None else outs


def _attn_kernel(q_ref, k_ref, v_ref, g_ref, o_ref, acc_ref, carry_ref, *, tile, heads):
    s_len = q_ref.shape[0]
    nt = s_len // tile
    row = lax.broadcasted_iota(jnp.int32, (tile, tile), 0)
    col = lax.broadcasted_iota(jnp.int32, (tile, tile), 1)
    causal = col < row
    neg_later = jnp.where(row > col, -1.0, 0.0).astype(BF16)
    for j in reversed(range(nt)):
        first = j == nt - 1
        lo, hi = j * tile, (j + 1) * tile
        for h in range(heads):
            cols = slice(h * HEAD_DIM, (h + 1) * HEAD_DIM)
            k = k_ref[lo:hi, cols]
            v = v_ref[lo:hi, cols]
            z = lax.dot_general(q_ref[lo:, cols], k, (((1,), (1,)), ((), ())), preferred_element_type=F32)
            sp = jnp.maximum(z, 0.0) + jnp.log2(1.0 + jnp.exp2(-jnp.abs(z)))
            sp_diag = jnp.where(causal, sp[:tile], 0.0)
            sp = sp_diag if first else jnp.concatenate([sp_diag, sp[tile:]], axis=0)
            suffix = jnp.dot(sp.astype(BF16), neg_later, preferred_element_type=F32)
            arg = (z - sp) + suffix
            if not first:
                carry = carry_ref[hi:, cols]
                carry = jnp.concatenate([carry] * (tile // LANES), axis=1)
                arg = jnp.concatenate([arg[:tile], arg[tile:] + carry], axis=0)
            w = jnp.exp2(arg)
            w_diag = jnp.where(causal, w[:tile], 0.0)
            w = w_diag if first else jnp.concatenate([w_diag, w[tile:]], axis=0)
            pv = jnp.dot(w.astype(BF16), v, preferred_element_type=F32)
            total = -jnp.broadcast_to(jnp.sum(sp, axis=1, keepdims=True), (sp.shape[0], LANES))
            acc_ref[lo:hi, cols] = pv[:tile]
            carry_ref[lo:hi, cols] = total[:tile]
            if not first:
                acc_ref[hi:, cols] += pv[tile:]
                carry_ref[hi:, cols] += total[tile:]
    o_ref[...] = (acc_ref[...] * g_ref[...].astype(F32)).astype(o_ref.dtype)


def _attention(q, k, v, gate, *, batch, seq):
    m, d = q.shape
    heads = 2 if d % (2 * HEAD_DIM) == 0 else 1
    width = heads * HEAD_DIM
    tile = min(256, seq)
    r3 = lambda a: a.reshape(batch, seq, d)
    spec = pl.BlockSpec((None, seq, width), lambda b, h: (b, 0, h))
    est = 10 * seq * width * 2 + 2 * seq * width * 4 + 8 * heads * seq * tile * 4
    out = pl.pallas_call(
        functools.partial(_attn_kernel, tile=tile, heads=heads),
        out_shape=jax.ShapeDtypeStruct((batch, seq, d), BF16),
        grid=(batch, d // width),
        in_specs=[spec, spec, spec, spec],
        out_specs=spec,
        scratch_shapes=[pltpu.VMEM((seq, width), F32),
                        pltpu.VMEM((seq, width), F32)],
        compiler_params=pltpu.CompilerParams(
            dimension_semantics=("parallel", "parallel"),
            vmem_limit_bytes=_vmem_limit(est + (8 << 20))),
        name="stickbreak_attn",
    )(r3(q), r3(k), r3(v), r3(gate))
    return out.reshape(m, d)


def _ssm_prep_kernel(p_ref, bre_ref, bim_ref, cre_ref, cim_ref, t_ref, w_ref, vt_ref, lam_ref):
    n_rows = 24
    n = lax.broadcasted_iota(jnp.int32, (n_rows, LANES), 0).astype(F32)
    lane = lax.broadcasted_iota(jnp.int32, (GROUP, LANES), 1)

    def per_group(gi, _):
        a_re = p_ref[gi, 0:1, :]
        a_im = p_ref[gi, 1:2, :]
        dt = jnp.exp(p_ref[gi, 2:3, :])
        half_mask = p_ref[gi, 3:4, :]
        mag = jnp.exp(n * (a_re * dt))
        ang = n * (a_im * dt)
        pre = mag * jnp.cos(ang)
        pim = mag * jnp.sin(ang)
        lam_re = pre[1:2, :]
        lam_im = pim[1:2, :]
        den = a_re * a_re + a_im * a_im
        f_re = ((lam_re - 1.0) * a_re + lam_im * a_im) / den
        f_im = (lam_im * a_re - (lam_re - 1.0) * a_im) / den
        fre = f_re * pre - f_im * pim
        fim = f_re * pim + f_im * pre
        bre = bre_ref[gi]
        bim = bim_ref[gi]
        cre = cre_ref[gi]
        cim = cim_ref[gi]

        v_blocks = []
        for e in range(CHUNK + 1):
            va = cre * pre[e:e + 1, :] - cim * pim[e:e + 1, :]
            vb = -(cre * pim[e:e + 1, :] + cim * pre[e:e + 1, :])
            v_blocks.append(jnp.concatenate([va, vb], axis=1))
        for t in range(CHUNK):
            vt_ref[gi, t * GROUP:(t + 1) * GROUP, :] = v_blocks[t + 1].astype(vt_ref.dtype)
        v0t = jnp.concatenate(v_blocks[:CHUNK], axis=0)

        for s in range(CHUNK):
            e = CHUNK - 1 - s
            wa = fre[e:e + 1, :] * bre - fim[e:e + 1, :] * bim
            wb = fre[e:e + 1, :] * bim + fim[e:e + 1, :] * bre
            w_ref[gi, s * GROUP:(s + 1) * GROUP, :] = jnp.concatenate([wa, wb], axis=1).astype(w_ref.dtype)

        wb0 = jnp.concatenate([fre[0:1, :] * bre - fim[0:1, :] * bim,
                               fre[0:1, :] * bim + fim[0:1, :] * bre], axis=1)
        kmat = lax.dot_general(wb0, v0t, (((1,), (1,)), ((), ())),
                               precision=lax.Precision.HIGHEST, preferred_element_type=F32)
        k0 = kmat[:, :LANES]
        k1 = kmat[:, LANES:]
        zero = jnp.zeros_like(k0)
        for s in range(CHUNK):
            sh = s * GROUP
            if s == 0:
                lo, hi = k0, k1
            elif sh < LANES:
                r0 = pltpu.roll(k0, sh, 1)
                r1 = pltpu.roll(k1, sh, 1)
                lo = jnp.where(lane >= sh, r0, 0.0)
                hi = jnp.where(lane >= sh, r1, r0)
            elif sh == LANES:
                lo, hi = zero, k0
            else:
                r0 = pltpu.roll(k0, sh - LANES, 1)
                lo = zero
                hi = jnp.where(lane >= sh - LANES, r0, 0.0)
            t_ref[gi, s * GROUP:(s + 1) * GROUP, :] = jnp.concatenate([lo, hi], axis=1).astype(t_ref.dtype)

        lam_rows = jnp.concatenate([pre[CHUNK:CHUNK + 1, :] * half_mask, pim[CHUNK:CHUNK + 1, :] * half_mask,
                                    jnp.zeros((6, LANES), F32)], axis=0)
        lam_ref[gi] = lam_rows
        return 0

    lax.fori_loop(0, p_ref.shape[0], per_group, 0, unroll=True)


def _ssm_prep(a_re, a_im, log_dt, b_re, b_im, c_re, c_im):
    g = a_re.shape[0]
    even = (jnp.arange(g) % 2 == 0)[:, None, None]

    def pad_half(a):
        z = jnp.zeros_like(a)
        return jnp.where(even, jnp.concatenate([a, z], -1), jnp.concatenate([z, a], -1))

    dup = lambda a: jnp.concatenate([a, a], -1)
    half_mask = jnp.where(even[:, 0], jnp.concatenate([jnp.ones((g, STATE)), jnp.zeros((g, STATE))], -1),
                          jnp.concatenate([jnp.zeros((g, STATE)), jnp.ones((g, STATE))], -1))
    params = jnp.stack([dup(a_re), dup(a_im), jnp.broadcast_to(log_dt[:, None], (g, LANES)), half_mask]
                       + [jnp.zeros((g, LANES), F32)] * 4, axis=1).astype(F32)
    bt_re = pad_half(jnp.swapaxes(b_re, 1, 2))
    bt_im = pad_half(jnp.swapaxes(b_im, 1, 2))
    cp_re = pad_half(c_re)
    cp_im = pad_half(c_im)
    gb = GROUPS_PER_TILE
    kk = CHUNK * GROUP
    spec8 = pl.BlockSpec((gb, 8, LANES), lambda i: (i, 0, 0))
    spec16 = pl.BlockSpec((gb, GROUP, LANES), lambda i: (i, 0, 0))
    specm = pl.BlockSpec((gb, kk, kk), lambda i: (i, 0, 0))
    return pl.pallas_call(
        _ssm_prep_kernel,
        out_shape=(jax.ShapeDtypeStruct((g, kk, kk), BF16),
                   jax.ShapeDtypeStruct((g, kk, 2 * LANES), BF16),
                   jax.ShapeDtypeStruct((g, kk, 2 * LANES), BF16),
                   jax.ShapeDtypeStruct((g, 8, LANES), F32)),
        grid=(g // gb,),
        in_specs=[spec8, spec16, spec16, spec16, spec16],
        out_specs=(specm, specm, specm, spec8),
        compiler_params=pltpu.CompilerParams(dimension_semantics=("parallel",)),
        name="ssm_prep",
    )(params, bt_re, bt_im, cp_re, cp_im)


def _ssm_kernel(u_ref, t_ref, w_ref, vt_ref, lam_ref, d_ref, o_ref, zs, ys, ss, hs, yscr, *, batch, nchunk,
                block_rows):
    n_rows = batch * nchunk
    pairs = GROUPS_PER_TILE // 2
    pitch = nchunk + SCAN_PITCH_PAD
    lane_group = lax.broadcasted_iota(jnp.int32, (block_rows, LANES), 1) // GROUP
    bit_set = {d: (lane_group & d) != 0 for d in (4, 2, 1)}

    def transpose_pieces(xs):
        for d in (4, 2, 1):
            new = list(xs)
            for m in range(GROUPS_PER_TILE):
                if m & d:
                    continue
                a, b = xs[m], xs[m + d]
                new[m] = jnp.where(bit_set[d], pltpu.roll(b, GROUP * d, 1), a)
                new[m + d] = jnp.where(bit_set[d], b, pltpu.roll(a, LANES - GROUP * d, 1))
            xs = new
        return xs

    def position_rows(r0, t):
        return pl.ds(r0 * CHUNK + t, block_rows, stride=CHUNK)

    row_starts = range(0, n_rows, block_rows)

    for r0 in row_starts:
        for half in range(2):
            pieces = transpose_pieces([u_ref[position_rows(r0, half * 8 + m), :] for m in range(GROUPS_PER_TILE)])
            for g in range(GROUPS_PER_TILE):
                zs[g, r0:r0 + block_rows, half * LANES:(half + 1) * LANES] = pieces[g].astype(zs.dtype)

    for p in range(pairs):
        s_end = (jnp.dot(zs[2 * p], w_ref[2 * p], preferred_element_type=F32)
                 + jnp.dot(zs[2 * p + 1], w_ref[2 * p + 1], preferred_element_type=F32))
        for b in range(batch):
            ss[2 * p, b * pitch:b * pitch + nchunk, :] = s_end[b * nchunk:(b + 1) * nchunk, :LANES]
            ss[2 * p + 1, b * pitch:b * pitch + nchunk, :] = s_end[b * nchunk:(b + 1) * nchunk, LANES:]

    lre = [lam_ref[2 * p, 0:1, :] + lam_ref[2 * p + 1, 0:1, :] for p in range(pairs)]
    lim = [lam_ref[2 * p, 1:2, :] + lam_ref[2 * p + 1, 1:2, :] for p in range(pairs)]

    def scan(c, carry):
        new = []
        for p in range(pairs):
            h_re, h_im = carry[2 * p], carry[2 * p + 1]
            rows = pl.ds(c, batch, stride=pitch)
            hs[2 * p, rows, :] = h_re
            hs[2 * p + 1, rows, :] = h_im
            s_re = ss[2 * p, rows, :]
            s_im = ss[2 * p + 1, rows, :]
            new.append(h_re * lre[p] - h_im * lim[p] + s_re)
            new.append(h_im * lre[p] + h_re * lim[p] + s_im)
        return tuple(new)

    zero_state = jnp.zeros((batch, LANES), F32)
    lax.fori_loop(0, nchunk, scan, tuple(zero_state for _ in range(2 * pairs)))

    def entering(q):
        return jnp.concatenate([hs[q, b * pitch:b * pitch + nchunk, :] for b in range(batch)], axis=0)

    for g in range(GROUPS_PER_TILE):
        h_in = jnp.concatenate([entering(2 * (g // 2)), entering(2 * (g // 2) + 1)], axis=1).astype(BF16)
        ys[g] = (jnp.dot(zs[g], t_ref[g], preferred_element_type=F32)
                 + lax.dot_general(h_in, vt_ref[g], (((1,), (1,)), ((), ())), preferred_element_type=F32))

    d_row = d_ref[...]
    for r0 in row_starts:
        for half in range(2):
            pieces = transpose_pieces([ys[g, r0:r0 + block_rows, half * LANES:(half + 1) * LANES]
                                       for g in range(GROUPS_PER_TILE)])
            for m in range(GROUPS_PER_TILE):
                rows = position_rows(r0, half * 8 + m)
                yscr[rows, :] = jax.nn.gelu(pieces[m] + d_row * u_ref[rows, :])

    o_ref[...] = yscr[...].astype(o_ref.dtype)


def _ssm(u, tmat, wmat, vtmat, lam, d_vec, *, batch, seq):
    m, d = u.shape
    nchunk = seq // CHUNK
    n_rows = batch * nchunk
    n_pad = batch * (nchunk + SCAN_PITCH_PAD)
    gb = GROUPS_PER_TILE
    kk = CHUNK * GROUP
    block_rows = min(n_rows, 128)
    specm = pl.BlockSpec((gb, kk, kk), lambda i: (i, 0, 0))
    est = (2 * m * LANES * (4 + 2) + 2 * 3 * gb * kk * kk * 2 + gb * n_rows * kk * (2 + 4)
           + 2 * gb * n_pad * LANES * 4 + m * LANES * 4 + 8 * n_rows * kk * 4)
    return pl.pallas_call(
        functools.partial(_ssm_kernel, batch=batch, nchunk=nchunk, block_rows=block_rows),
        out_shape=jax.ShapeDtypeStruct((m, d), BF16),
        grid=(d // LANES,),
        in_specs=[pl.BlockSpec((m, LANES), lambda i: (0, i)),
                  specm, specm, specm,
                  pl.BlockSpec((gb, 8, LANES), lambda i: (i, 0, 0)),
                  pl.BlockSpec((1, LANES), lambda i: (0, i))],
        out_specs=pl.BlockSpec((m, LANES), lambda i: (0, i)),
        scratch_shapes=[pltpu.VMEM((gb, n_rows, kk), BF16),
                        pltpu.VMEM((gb, n_rows, kk), F32),
                        pltpu.VMEM((gb, n_pad, LANES), F32),
                        pltpu.VMEM((gb, n_pad, LANES), F32),
                        pltpu.VMEM((m, LANES), F32)],
        compiler_params=pltpu.CompilerParams(
            dimension_semantics=("parallel",),
            vmem_limit_bytes=_vmem_limit(est)),
        name="s5_scan",
    )(u, tmat, wmat, vtmat, lam, d_vec.reshape(1, d))


def kernel(x, norm_g, attn_w_in, attn_q_g, attn_k_g, attn_w_out, ssm_w_in, ssm_A_re, ssm_A_im, ssm_log_dt,
           ssm_B_re, ssm_B_im, ssm_C_re, ssm_C_im, ssm_D, ssm_glu_w, ssm_glu_b, ssm_w_out):
    batch, seq, d = x.shape
    m = batch * seq
    assert d % LANES == 0 and seq % 256 == 0 and (m // CHUNK) % 8 == 0
    assert ssm_A_re.shape[1:] == (d // GROUP, STATE) and (d // GROUP) % GROUPS_PER_TILE == 0
    x2 = x.reshape(m, d)

    h = _rmsnorm(x2, norm_g[0])
    w_in = attn_w_in[0]
    qg = attn_q_g[0].reshape(1, HEAD_DIM)
    kg = attn_k_g[0].reshape(1, HEAD_DIM)
    q = _matmul(h, w_in, col_off=0, n=d, mode="headnorm", out_dtype=BF16, extras=(qg,),
                scale=LOG2E / math.sqrt(HEAD_DIM))
    k = _matmul(h, w_in, col_off=d, n=d, mode="headnorm", out_dtype=BF16, extras=(kg,))
    v = _matmul(h, w_in, col_off=2 * d, n=d, mode="store", out_dtype=BF16)
    gate = _matmul(h, w_in, col_off=3 * d, n=d, mode="silu", out_dtype=BF16)
    og = _attention(q, k, v, gate, batch=batch, seq=seq)
    x2, h, ssq = _matmul(og, attn_w_out[0], col_off=0, n=d, mode="residual", out_dtype=F32, extras=(x2,),
                         next_gain=norm_g[1])

    w_in = ssm_w_in[0]
    u = _matmul(h, w_in, col_off=0, n=d, mode="store", out_dtype=F32, row_ssq=ssq)
    gate = _matmul(h, w_in, col_off=d, n=d, mode="silu", out_dtype=BF16, row_ssq=ssq)
    tmat, wmat, vtmat, lam = _ssm_prep(ssm_A_re[0], ssm_A_im[0], ssm_log_dt[0], ssm_B_re[0], ssm_B_im[0],
                                       ssm_C_re[0], ssm_C_im[0])
    y = _ssm(u, tmat, wmat, vtmat, lam, ssm_D[0], batch=batch, seq=seq)
    yg = _matmul(y, ssm_glu_w[0], col_off=0, n=d, mode="glu", out_dtype=BF16,
                 extras=(ssm_glu_b[0].reshape(1, d), y, gate))
    x2 = _matmul(yg, ssm_w_out[0], col_off=0, n=d, mode="residual", out_dtype=F32, extras=(x2,))
    return x2.reshape(batch, seq, d)
```

```python
import functools
import math

import jax
import jax.numpy as jnp
from jax import lax
from jax.experimental import pallas as pl
from jax.experimental.pallas import tpu as pltpu

HEAD_DIM = 128
GROUP = 16
STATE = 64
RMS_EPS = 1e-6
LANES = 128
CHUNK = 16
LOG2E = 1.4426950408889634
SCAN_PITCH_PAD = 8
GROUPS_PER_TILE = LANES // GROUP
VMEM_CAP_BYTES = 60 * 1024 * 1024

F32 = jnp.float32
BF16 = jnp.bfloat16


def _vmem_limit(estimate_bytes):
    return int(min(VMEM_CAP_BYTES, max(32 * 1024 * 1024, estimate_bytes)))


def _rmsnorm_kernel(x_ref, g_ref, o_ref):
    x = x_ref[...]
    r = lax.rsqrt(jnp.mean(x * x, axis=-1, keepdims=True) + RMS_EPS)
    o_ref[...] = ((x * r) * g_ref[...]).astype(o_ref.dtype)


def _rmsnorm(x2, g):
    m, d = x2.shape
    tm = min(512, m)
    return pl.pallas_call(
        _rmsnorm_kernel,
        out_shape=jax.ShapeDtypeStruct((m, d), BF16),
        grid=(m // tm,),
        in_specs=[pl.BlockSpec((tm, d), lambda i: (i, 0)),
                  pl.BlockSpec((1, d), lambda i: (0, 0))],
        out_specs=pl.BlockSpec((tm, d), lambda i: (i, 0)),
        compiler_params=pltpu.CompilerParams(dimension_semantics=("parallel",),
                                             vmem_limit_bytes=_vmem_limit(2 * tm * d * (4 + 2) + 4 * tm * d * 4)),
        name="rmsnorm",
    )(x2, g.reshape(1, d))


def _mm_epilogue(acc, extras, outs, *, mode, scale, row_ssq, emit_norm):
    extras = list(extras)
    if row_ssq:
        ssq = extras.pop(0)[...]
        total = sum(ssq[:, c * LANES:(c + 1) * LANES] for c in range(ssq.shape[1] // LANES))
        r = lax.rsqrt(total * (1.0 / row_ssq) + RMS_EPS)
        acc = acc * jnp.concatenate([r] * (acc.shape[1] // LANES), axis=1)
    o_ref = outs[0]
    if mode == "headnorm":
        g = extras[0][...]
        for h in range(acc.shape[1] // HEAD_DIM):
            sl = slice(h * HEAD_DIM, (h + 1) * HEAD_DIM)
            a = acc[:, sl]
            r = lax.rsqrt(jnp.mean(a * a, axis=-1, keepdims=True) + RMS_EPS)
            o_ref[:, sl] = (((a * r) * g) * scale).astype(o_ref.dtype)
    elif mode == "silu":
        o_ref[...] = (acc * jax.nn.sigmoid(acc)).astype(o_ref.dtype)
    elif mode == "residual":
        y = extras[0][...] + acc
        o_ref[...] = y.astype(o_ref.dtype)
        if emit_norm:
            outs[1][...] = (y * extras[1][...]).astype(outs[1].dtype)
            outs[2][...] = jnp.broadcast_to(jnp.sum(y * y, axis=1, keepdims=True), outs[2].shape)
    elif mode == "glu":
        b_ref, y_ref, gate_ref = extras[0], extras[1], extras[2]
        t = acc + b_ref[...]
        o_ref[...] = (y_ref[...].astype(F32) * jax.nn.sigmoid(t) * gate_ref[...].astype(F32)).astype(o_ref.dtype)
    else:
        o_ref[...] = acc.astype(o_ref.dtype)


def _mm_kernel(x_ref, w_ref, *rest, mode, scale, n_out, row_ssq, emit_norm):
    acc = jnp.dot(x_ref[...], w_ref[...].astype(BF16), preferred_element_type=F32)
    _mm_epilogue(acc, rest[:-n_out], rest[-n_out:], mode=mode, scale=scale, row_ssq=row_ssq, emit_norm=emit_norm)


def _matmul(x, w, *, col_off, n, mode, out_dtype, extras=(), scale=1.0, tm=512, tn=1024, row_ssq=None,
            next_gain=None):
    m, k = x.shape
    tm = min(tm, m)
    tn = min(tn, n)
    off = col_off // tn
    col_tiles = n // tn
    tile_spec = pl.BlockSpec((tm, tn), lambda j, i: (i, j))
    in_specs = [pl.BlockSpec((tm, k), lambda j, i: (i, 0)),
                pl.BlockSpec((k, tn), lambda j, i: (0, j + off))]
    operands = [x, w]
    extra_bytes = 0
    if row_ssq is not None:
        operands.append(row_ssq)
        in_specs.append(pl.BlockSpec((tm, row_ssq.shape[1]), lambda j, i: (i, 0)))
    extras = tuple(extras) + ((next_gain.reshape(1, n),) if next_gain is not None else ())
    for e in extras:
        operands.append(e)
        if e.shape[0] == 1:
            width = e.shape[1]
            if width == n:
                in_specs.append(pl.BlockSpec((1, tn), lambda j, i: (0, j)))
            else:
                in_specs.append(pl.BlockSpec((1, width), lambda j, i: (0, 0)))
        else:
            in_specs.append(tile_spec)
            extra_bytes += tm * tn * e.dtype.itemsize
    out_shape = [jax.ShapeDtypeStruct((m, n), out_dtype)]
    out_specs = [tile_spec]
    if next_gain is not None:
        out_shape += [jax.ShapeDtypeStruct((m, n), BF16), jax.ShapeDtypeStruct((m, col_tiles * LANES), F32)]
        out_specs += [tile_spec, pl.BlockSpec((tm, LANES), lambda j, i: (i, j))]
        extra_bytes += tm * tn * 2
    out_bytes = tm * tn * jnp.dtype(out_dtype).itemsize
    est = (2 * (tm * k * 2 + k * tn * w.dtype.itemsize + out_bytes + extra_bytes)
           + k * tn * 2 + 3 * tm * tn * 4)
    outs = pl.pallas_call(
        functools.partial(_mm_kernel, mode=mode, scale=scale, n_out=len(out_shape),
                          row_ssq=(k if row_ssq is not None else None), emit_norm=next_gain is not None),
        out_shape=out_shape,
        grid=(col_tiles, m // tm),
        in_specs=in_specs,
        out_specs=out_specs,
        compiler_params=pltpu.CompilerParams(
            dimension_semantics=("parallel", "parallel"),
            vmem_limit_bytes=_vmem_limit(est + (4 << 20))),
        name="mm_" + mode,
    )(*operands)
    return outs[0] if next_gain is None else outs


def _attn_kernel(q_ref, k_ref, v_ref, g_ref, o_ref, acc_ref, carry_ref, *, tile, heads):
    s_len = q_ref.shape[0]
    nt = s_len // tile
    row = lax.broadcasted_iota(jnp.int32, (tile, tile), 0)
    col = lax.broadcasted_iota(jnp.int32, (tile, tile), 1)
    causal = col < row
    neg_later = jnp.where(row > col, -1.0, 0.0).astype(BF16)
    for j in reversed(range(nt)):
        first = j == nt - 1
        lo, hi = j * tile, (j + 1) * tile
        for h in range(heads):
            cols = slice(h * HEAD_DIM, (h + 1) * HEAD_DIM)
            k = k_ref[lo:hi, cols]
            v = v_ref[lo:hi, cols]
            z = lax.dot_general(q_ref[lo:, cols], k, (((1,), (1,)), ((), ())), preferred_element_type=F32)
            sp = jnp.maximum(z, 0.0) + jnp.log2(1.0 + jnp.exp2(-jnp.abs(z)))
            sp_diag = jnp.where(causal, sp[:tile], 0.0)
            sp = sp_diag if first else jnp.concatenate([sp_diag, sp[tile:]], axis=0)
            suffix = jnp.dot(sp.astype(BF16), neg_later, preferred_element_type=F32)
            arg = (z - sp) + suffix
            if not first:
                carry = carry_ref[hi:, cols]
                carry = jnp.concatenate([carry] * (tile // LANES), axis=1)
                arg = jnp.concatenate([arg[:tile], arg[tile:] + carry], axis=0)
            w = jnp.exp2(arg)
            w_diag = jnp.where(causal, w[:tile], 0.0)
            w = w_diag if first else jnp.concatenate([w_diag, w[tile:]], axis=0)
            pv = jnp.dot(w.astype(BF16), v, preferred_element_type=F32)
            total = -jnp.broadcast_to(jnp.sum(sp, axis=1, keepdims=True), (sp.shape[0], LANES))
            acc_ref[lo:hi, cols] = pv[:tile]
            carry_ref[lo:hi, cols] = total[:tile]
            if not first:
                acc_ref[hi:, cols] += pv[tile:]
                carry_ref[hi:, cols] += total[tile:]
    o_ref[...] = (acc_ref[...] * g_ref[...].astype(F32)).astype(o_ref.dtype)


def _attention(q, k, v, gate, *, batch, seq):
    m, d = q.shape
    heads = 2 if d % (2 * HEAD_DIM) == 0 else 1
    width = heads * HEAD_DIM
    tile = min(256, seq)
    r3 = lambda a: a.reshape(batch, seq, d)
    spec = pl.BlockSpec((None, seq, width), lambda b, h: (b, 0, h))
    est = 10 * seq * width * 2 + 2 * seq * width * 4 + 8 * heads * seq * tile * 4
    out = pl.pallas_call(
        functools.partial(_attn_kernel, tile=tile, heads=heads),
        out_shape=jax.ShapeDtypeStruct((batch, seq, d), BF16),
        grid=(batch, d // width),
        in_specs=[spec, spec, spec, spec],
        out_specs=spec,
        scratch_shapes=[pltpu.VMEM((seq, width), F32),
                        pltpu.VMEM((seq, width), F32)],
        compiler_params=pltpu.CompilerParams(
            dimension_semantics=("parallel", "parallel"),
            vmem_limit_bytes=_vmem_limit(est + (8 << 20))),
        name="stickbreak_attn",
    )(r3(q), r3(k), r3(v), r3(gate))
    return out.reshape(m, d)


def _ssm_prep_kernel(p_ref, bre_ref, bim_ref, cre_ref, cim_ref, t_ref, w_ref, vt_ref, lam_ref):
    n_rows = 24
    n = lax.broadcasted_iota(jnp.int32, (n_rows, LANES), 0).astype(F32)
    lane = lax.broadcasted_iota(jnp.int32, (GROUP, LANES), 1)

    def per_group(gi, _):
        a_re = p_ref[gi, 0:1, :]
        a_im = p_ref[gi, 1:2, :]
        dt = jnp.exp(p_ref[gi, 2:3, :])
        half_mask = p_ref[gi, 3:4, :]
        mag = jnp.exp(n * (a_re * dt))
        ang = n * (a_im * dt)
        pre = mag * jnp.cos(ang)
        pim = mag * jnp.sin(ang)
        lam_re = pre[1:2, :]
        lam_im = pim[1:2, :]
        den = a_re * a_re + a_im * a_im
        f_re = ((lam_re - 1.0) * a_re + lam_im * a_im) / den
        f_im = (lam_im * a_re - (lam_re - 1.0) * a_im) / den
        fre = f_re * pre - f_im * pim
        fim = f_re * pim + f_im * pre
        bre = bre_ref[gi]
        bim = bim_ref[gi]
        cre = cre_ref[gi]
        cim = cim_ref[gi]

        v_blocks = []
        for e in range(CHUNK + 1):
            va = cre * pre[e:e + 1, :] - cim * pim[e:e + 1, :]
            vb = -(cre * pim[e:e + 1, :] + cim * pre[e:e + 1, :])
            v_blocks.append(jnp.concatenate([va, vb], axis=1))
        for t in range(CHUNK):
            vt_ref[gi, t * GROUP:(t + 1) * GROUP, :] = v_blocks[t + 1].astype(vt_ref.dtype)
        v0t = jnp.concatenate(v_blocks[:CHUNK], axis=0)

        for s in range(CHUNK):
            e = CHUNK - 1 - s
            wa = fre[e:e + 1, :] * bre - fim[e:e + 1, :] * bim
            wb = fre[e:e + 1, :] * bim + fim[e:e + 1, :] * bre
            w_ref[gi, s * GROUP:(s + 1) * GROUP, :] = jnp.concatenate([wa, wb], axis=1).astype(w_ref.dtype)

        wb0 = jnp.concatenate([fre[0:1, :] * bre - fim[0:1, :] * bim,
                               fre[0:1, :] * bim + fim[0:1, :] * bre], axis=1)
        kmat = lax.dot_general(wb0, v0t, (((1,), (1,)), ((), ())),
                               precision=lax.Precision.HIGHEST, preferred_element_type=F32)
        k0 = kmat[:, :LANES]
        k1 = kmat[:, LANES:]
        zero = jnp.zeros_like(k0)
        for s in range(CHUNK):
            sh = s * GROUP
            if s == 0:
                lo, hi = k0, k1
            elif sh < LANES:
                r0 = pltpu.roll(k0, sh, 1)
                r1 = pltpu.roll(k1, sh, 1)
                lo = jnp.where(lane >= sh, r0, 0.0)
                hi = jnp.where(lane >= sh, r1, r0)
            elif sh == LANES:
                lo, hi = zero, k0
            else:
                r0 = pltpu.roll(k0, sh - LANES, 1)
                lo = zero
                hi = jnp.where(lane >= sh - LANES, r0, 0.0)
            t_ref[gi, s * GROUP:(s + 1) * GROUP, :] = jnp.concatenate([lo, hi], axis=1).astype(t_ref.dtype)

        lam_rows = jnp.concatenate([pre[CHUNK:CHUNK + 1, :] * half_mask, pim[CHUNK:CHUNK + 1, :] * half_mask,
                                    jnp.zeros((6, LANES), F32)], axis=0)
        lam_ref[gi] = lam_rows
        return 0

    lax.fori_loop(0, p_ref.shape[0], per_group, 0, unroll=True)


def _ssm_prep(a_re, a_im, log_dt, b_re, b_im, c_re, c_im):
    g = a_re.shape[0]
    even = (jnp.arange(g) % 2 == 0)[:, None, None]

    def pad_half(a):
        z = jnp.zeros_like(a)
        return jnp.where(even, jnp.concatenate([a, z], -1), jnp.concatenate([z, a], -1))

    dup = lambda a: jnp.concatenate([a, a], -1)
    half_mask = jnp.where(even[:, 0], jnp.concatenate([jnp.ones((g, STATE)), jnp.zeros((g, STATE))], -1),
                          jnp.concatenate([jnp.zeros((g, STATE)), jnp.ones((g, STATE))], -1))
    params = jnp.stack([dup(a_re), dup(a_im), jnp.broadcast_to(log_dt[:, None], (g, LANES)), half_mask]
                       + [jnp.zeros((g, LANES), F32)] * 4, axis=1).astype(F32)
    bt_re = pad_half(jnp.swapaxes(b_re, 1, 2))
    bt_im = pad_half(jnp.swapaxes(b_im, 1, 2))
    cp_re = pad_half(c_re)
    cp_im = pad_half(c_im)
    gb = GROUPS_PER_TILE
    kk = CHUNK * GROUP
    spec8 = pl.BlockSpec((gb, 8, LANES), lambda i: (i, 0, 0))
    spec16 = pl.BlockSpec((gb, GROUP, LANES), lambda i: (i, 0, 0))
    specm = pl.BlockSpec((gb, kk, kk), lambda i: (i, 0, 0))
    return pl.pallas_call(
        _ssm_prep_kernel,
        out_shape=(jax.ShapeDtypeStruct((g, kk, kk), BF16),
                   jax.ShapeDtypeStruct((g, kk, 2 * LANES), BF16),
                   jax.ShapeDtypeStruct((g, kk, 2 * LANES), BF16),
                   jax.ShapeDtypeStruct((g, 8, LANES), F32)),
        grid=(g // gb,),
        in_specs=[spec8, spec16, spec16, spec16, spec16],
        out_specs=(specm, specm, specm, spec8),
        compiler_params=pltpu.CompilerParams(dimension_semantics=("parallel",)),
        name="ssm_prep",
    )(params, bt_re, bt_im, cp_re, cp_im)


def _ssm_kernel(u_ref, t_ref, w_ref, vt_ref, lam_ref, d_ref, o_ref, zs, ys, ss, hs, yscr, *, batch, nchunk,
                block_rows):
    n_rows = batch * nchunk
    pairs = GROUPS_PER_TILE // 2
    pitch = nchunk + SCAN_PITCH_PAD
    lane_group = lax.broadcasted_iota(jnp.int32, (block_rows, LANES), 1) // GROUP
    bit_set = {d: (lane_group & d) != 0 for d in (4, 2, 1)}

    def transpose_pieces(xs):
        for d in (4, 2, 1):
            new = list(xs)
            for m in range(GROUPS_PER_TILE):
                if m & d:
                    continue
                a, b = xs[m], xs[m + d]
                new[m] = jnp.where(bit_set[d], pltpu.roll(b, GROUP * d, 1), a)
                new[m + d] = jnp.where(bit_set[d], b, pltpu.roll(a, LANES - GROUP * d, 1))
            xs = new
        return xs

    def position_rows(r0, t):
        return pl.ds(r0 * CHUNK + t, block_rows, stride=CHUNK)

    row_starts = range(0, n_rows, block_rows)

    for r0 in row_starts:
        for half in range(2):
            pieces = transpose_pieces([u_ref[position_rows(r0, half * 8 + m), :] for m in range(GROUPS_PER_TILE)])
            for g in range(GROUPS_PER_TILE):
                zs[g, r0:r0 + block_rows, half * LANES:(half + 1) * LANES] = pieces[g].astype(zs.dtype)

    for p in range(pairs):
        s_end = (jnp.dot(zs[2 * p], w_ref[2 * p], preferred_element_type=F32)
                 + jnp.dot(zs[2 * p + 1], w_ref[2 * p + 1], preferred_element_type=F32))
        for b in range(batch):
            ss[2 * p, b * pitch:b * pitch + nchunk, :] = s_end[b * nchunk:(b + 1) * nchunk, :LANES]
            ss[2 * p + 1, b * pitch:b * pitch + nchunk, :] = s_end[b * nchunk:(b + 1) * nchunk, LANES:]

    lre = [lam_ref[2 * p, 0:1, :] + lam_ref[2 * p + 1, 0:1, :] for p in range(pairs)]
    lim = [lam_ref[2 * p, 1:2, :] + lam_ref[2 * p + 1, 1:2, :] for p in range(pairs)]

    def scan(c, carry):
        new = []
        for p in range(pairs):
            h_re, h_im = carry[2 * p], carry[2 * p + 1]
            rows = pl.ds(c, batch, stride=pitch)
            hs[2 * p, rows, :] = h_re
            hs[2 * p + 1, rows, :] = h_im
            s_re = ss[2 * p, rows, :]
            s_im = ss[2 * p + 1, rows, :]
            new.append(h_re * lre[p] - h_im * lim[p] + s_re)
            new.append(h_im * lre[p] + h_re * lim[p] + s_im)
        return tuple(new)

    zero_state = jnp.zeros((batch, LANES), F32)
    lax.fori_loop(0, nchunk, scan, tuple(zero_state for _ in range(2 * pairs)))

    def entering(q):
        return jnp.concatenate([hs[q, b * pitch:b * pitch + nchunk, :] for b in range(batch)], axis=0)

    for g in range(GROUPS_PER_TILE):
        h_in = jnp.concatenate([entering(2 * (g // 2)), entering(2 * (g // 2) + 1)], axis=1).astype(BF16)
        ys[g] = (jnp.dot(zs[g], t_ref[g], preferred_element_type=F32)
                 + lax.dot_general(h_in, vt_ref[g], (((1,), (1,)), ((), ())), preferred_element_type=F32))

    d_row = d_ref[...]
    for r0 in row_starts:
        for half in range(2):
            pieces = transpose_pieces([ys[g, r0:r0 + block_rows, half * LANES:(half + 1) * LANES]
                                       for g in range(GROUPS_PER_TILE)])
            for m in range(GROUPS_PER_TILE):
                rows = position_rows(r0, half * 8 + m)
                yscr[rows, :] = jax.nn.gelu(pieces[m] + d_row * u_ref[rows, :])

    o_ref[...] = yscr[...].astype(o_ref.dtype)


def _ssm(u, tmat, wmat, vtmat, lam, d_vec, *, batch, seq):
    m, d = u.shape
    nchunk = seq // CHUNK
    n_rows = batch * nchunk
    n_pad = batch * (nchunk + SCAN_PITCH_PAD)
    gb = GROUPS_PER_TILE
    kk = CHUNK * GROUP
    block_rows = min(n_rows, 128)
    specm = pl.BlockSpec((gb, kk, kk), lambda i: (i, 0, 0))
    est = (2 * m * LANES * (4 + 2) + 2 * 3 * gb * kk * kk * 2 + gb * n_rows * kk * (2 + 4)
           + 2 * gb * n_pad * LANES * 4 + m * LANES * 4 + 8 * n_rows * kk * 4)
    return pl.pallas_call(
        functools.partial(_ssm_kernel, batch=batch, nchunk=nchunk, block_rows=block_rows),
        out_shape=jax.ShapeDtypeStruct((m, d), BF16),
        grid=(d // LANES,),
        in_specs=[pl.BlockSpec((m, LANES), lambda i: (0, i)),
                  specm, specm, specm,
                  pl.BlockSpec((gb, 8, LANES), lambda i: (i, 0, 0)),
                  pl.BlockSpec((1, LANES), lambda i: (0, i))],
        out_specs=pl.BlockSpec((m, LANES), lambda i: (0, i)),
        scratch_shapes=[pltpu.VMEM((gb, n_rows, kk), BF16),
                        pltpu.VMEM((gb, n_rows, kk), F32),
                        pltpu.VMEM((gb, n_pad, LANES), F32),
                        pltpu.VMEM((gb, n_pad, LANES), F32),
                        pltpu.VMEM((m, LANES), F32)],
        compiler_params=pltpu.CompilerParams(
            dimension_semantics=("parallel",),
            vmem_limit_bytes=_vmem_limit(est)),
        name="s5_scan",
    )(u, tmat, wmat, vtmat, lam, d_vec.reshape(1, d))


def kernel(x, norm_g, attn_w_in, attn_q_g, attn_k_g, attn_w_out, ssm_w_in, ssm_A_re, ssm_A_im, ssm_log_dt,
           ssm_B_re, ssm_B_im, ssm_C_re, ssm_C_im, ssm_D, ssm_glu_w, ssm_glu_b, ssm_w_out):
    batch, seq, d = x.shape
    m = batch * seq
    assert d % LANES == 0 and seq % 256 == 0 and (m // CHUNK) % 8 == 0
    assert ssm_A_re.shape[1:] == (d // GROUP, STATE) and (d // GROUP) % GROUPS_PER_TILE == 0
    x2 = x.reshape(m, d)

    h = _rmsnorm(x2, norm_g[0])
    w_in = attn_w_in[0]
    qg = attn_q_g[0].reshape(1, HEAD_DIM)
    kg = attn_k_g[0].reshape(1, HEAD_DIM)
    q = _matmul(h, w_in, col_off=0, n=d, mode="headnorm", out_dtype=BF16, extras=(qg,),
                scale=LOG2E / math.sqrt(HEAD_DIM))
    k = _matmul(h, w_in, col_off=d, n=d, mode="headnorm", out_dtype=BF16, extras=(kg,))
    v = _matmul(h, w_in, col_off=2 * d, n=d, mode="store", out_dtype=BF16)
    gate = _matmul(h, w_in, col_off=3 * d, n=d, mode="silu", out_dtype=BF16)
    og = _attention(q, k, v, gate, batch=batch, seq=seq)
    x2, h, ssq = _matmul(og, attn_w_out[0], col_off=0, n=d, mode="residual", out_dtype=F32, extras=(x2,),
                         next_gain=norm_g[1])

    w_in = ssm_w_in[0]
    u = _matmul(h, w_in, col_off=0, n=d, mode="store", out_dtype=F32, row_ssq=ssq)
    gate = _matmul(h, w_in, col_off=d, n=d, mode="silu", out_dtype=BF16, row_ssq=ssq)
    tmat, wmat, vtmat, lam = _ssm_prep(ssm_A_re[0], ssm_A_im[0], ssm_log_dt[0], ssm_B_re[0], ssm_B_im[0],
                                       ssm_C_re[0], ssm_C_im[0])
    y = _ssm(u, tmat, wmat, vtmat, lam, ssm_D[0], batch=batch, seq=seq)
    yg = _matmul(y, ssm_glu_w[0], col_off=0, n=d, mode="glu", out_dtype=BF16,
                 extras=(ssm_glu_b[0].reshape(1, d), y, gate))
    x2 = _matmul(yg, ssm_w_out[0], col_off=0, n=d, mode="residual", out_dtype=F32, extras=(x2,))
    return x2.reshape(batch, seq, d)
```

```python
import functools
import math

import jax
import jax.numpy as jnp
from jax import lax
from jax.experimental import pallas as pl
from jax.experimental.pallas import tpu as pltpu

HEAD_DIM = 128
GROUP = 16
STATE = 64
RMS_EPS = 1e-6
LANES = 128
CHUNK = 16
LOG2E = 1.4426950408889634
SCAN_PITCH_PAD = 8
GROUPS_PER_TILE = LANES // GROUP
VMEM_CAP_BYTES = 60 * 1024 * 1024

F32 = jnp.float32
BF16 = jnp.bfloat16


def _vmem_limit(estimate_bytes):
    return int(min(VMEM_CAP_BYTES, max(32 * 1024 * 1024, estimate_bytes)))


def _rmsnorm_kernel(x_ref, g_ref, o_ref):
    x = x_ref[...]
    r = lax.rsqrt(jnp.mean(x * x, axis=-1, keepdims=True) + RMS_EPS)
    o_ref[...] = ((x * r) * g_ref[...]).astype(o_ref.dtype)


def _rmsnorm(x2, g):
    m, d = x2.shape
    tm = min(512, m)
    return pl.pallas_call(
        _rmsnorm_kernel,
        out_shape=jax.ShapeDtypeStruct((m, d), BF16),
        grid=(m // tm,),
        in_specs=[pl.BlockSpec((tm, d), lambda i: (i, 0)),
                  pl.BlockSpec((1, d), lambda i: (0, 0))],
        out_specs=pl.BlockSpec((tm, d), lambda i: (i, 0)),
        compiler_params=pltpu.CompilerParams(dimension_semantics=("parallel",),
                                             vmem_limit_bytes=_vmem_limit(2 * tm * d * (4 + 2) + 4 * tm * d * 4)),
        name="rmsnorm",
    )(x2, g.reshape(1, d))


def _mm_epilogue(acc, extras, outs, *, mode, scale, row_ssq, emit_norm):
    extras = list(extras)
    if row_ssq:
        ssq = extras.pop(0)[...]
        total = sum(ssq[:, c * LANES:(c + 1) * LANES] for c in range(ssq.shape[1] // LANES))
        r = lax.rsqrt(total * (1.0 / row_ssq) + RMS_EPS)
        acc = acc * jnp.concatenate([r] * (acc.shape[1] // LANES), axis=1)
    o_ref = outs[0]
    if mode == "headnorm":
        g = extras[0][...]
        for h in range(acc.shape[1] // HEAD_DIM):
            sl = slice(h * HEAD_DIM, (h + 1) * HEAD_DIM)
            a = acc[:, sl]
            r = lax.rsqrt(jnp.mean(a * a, axis=-1, keepdims=True) + RMS_EPS)
            o_ref[:, sl] = (((a * r) * g) * scale).astype(o_ref.dtype)
    elif mode == "silu":
        o_ref[...] = (acc * jax.nn.sigmoid(acc)).astype(o_ref.dtype)
    elif mode == "residual":
        y = extras[0][...] + acc
        o_ref[...] = y.astype(o_ref.dtype)
        if emit_norm:
            outs[1][...] = (y * extras[1][...]).astype(outs[1].dtype)
            outs[2][...] = jnp.broadcast_to(jnp.sum(y * y, axis=1, keepdims=True), outs[2].shape)
    elif mode == "glu":
        b_ref, y_ref, gate_ref = extras[0], extras[1], extras[2]
        t = acc + b_ref[...]
        o_ref[...] = (y_ref[...].astype(F32) * jax.nn.sigmoid(t) * gate_ref[...].astype(F32)).astype(o_ref.dtype)
    else:
        o_ref[...] = acc.astype(o_ref.dtype)


def _mm_kernel(x_ref, w_ref, *rest, mode, scale, n_out, row_ssq, emit_norm):
    acc = jnp.dot(x_ref[...], w_ref[...].astype(BF16), preferred_element_type=F32)
    _mm_epilogue(acc, rest[:-n_out], rest[-n_out:], mode=mode, scale=scale, row_ssq=row_ssq, emit_norm=emit_norm)


def _matmul(x, w, *, col_off, n, mode, out_dtype, extras=(), scale=1.0, tm=512, tn=1024, row_ssq=None,
            next_gain=None):
    m, k = x.shape
    tm = min(tm, m)
    tn = min(tn, n)
    off = col_off // tn
    col_tiles = n // tn
    tile_spec = pl.BlockSpec((tm, tn), lambda j, i: (i, j))
    in_specs = [pl.BlockSpec((tm, k), lambda j, i: (i, 0)),
                pl.BlockSpec((k, tn), lambda j, i: (0, j + off))]
    operands = [x, w]
    extra_bytes = 0
    if row_ssq is not None:
        operands.append(row_ssq)
        in_specs.append(pl.BlockSpec((tm, row_ssq.shape[1]), lambda j, i: (i, 0)))
    extras = tuple(extras) + ((next_gain.reshape(1, n),) if next_gain is not None else ())
    for e in extras:
        operands.append(e)
        if e.shape[0] == 1:
            width = e.shape[1]
            if width == n:
                in_specs.append(pl.BlockSpec((1, tn), lambda j, i: (0, j)))
            else:
                in_specs.append(pl.BlockSpec((1, width), lambda j, i: (0, 0)))
        else:
            in_specs.append(tile_spec)
            extra_bytes += tm * tn * e.dtype.itemsize
    out_shape = [jax.ShapeDtypeStruct((m, n), out_dtype)]
    out_specs = [tile_spec]
    if next_gain is not None:
        out_shape += [jax.ShapeDtypeStruct((m, n), BF16), jax.ShapeDtypeStruct((m, col_tiles * LANES), F32)]
        out_specs += [tile_spec, pl.BlockSpec((tm, LANES), lambda j, i: (i, j))]
        extra_bytes += tm * tn * 2
    out_bytes = tm * tn * jnp.dtype(out_dtype).itemsize
    est = (2 * (tm * k * 2 + k * tn * w.dtype.itemsize + out_bytes + extra_bytes)
           + k * tn * 2 + 3 * tm * tn * 4)
    outs = pl.pallas_call(
        functools.partial(_mm_kernel, mode=mode, scale=scale, n_out=len(out_shape),
                          row_ssq=(k if row_ssq is not None else None), emit_norm=next_gain is not None),
        out_shape=out_shape,
        grid=(col_tiles, m // tm),
        in_specs=in_specs,
        out_specs=out_specs,
        compiler_params=pltpu.CompilerParams(
            dimension_semantics=("parallel", "parallel"),
            vmem_limit_bytes=_vmem_limit(est + (4 << 20))),
        name="mm_" + mode,
    )(*operands)
    return outs[0] if next_gain is None else outs


def _attn_kernel(q_ref, k_ref, v_ref, g_ref, o_ref, acc_ref, carry_ref, *, tile, heads):
    s_len = q_ref.shape[0]
    nt = s_len // tile
    row = lax.broadcasted_iota(jnp.int32, (tile, tile), 0)
    col = lax.broadcasted_iota(jnp.int32, (tile, tile), 1)
    causal = col < row
    neg_later = jnp.where(row > col, -1.0, 0.0).astype(BF16)
    for j in reversed(range(nt)):
        first = j == nt - 1
        lo, hi = j * tile, (j + 1) * tile
        for h in range(heads):
            cols = slice(h * HEAD_DIM, (h + 1) * HEAD_DIM)
            k = k_ref[lo:hi, cols]
            v = v_ref[lo:hi, cols]
            z = lax.dot_general(q_ref[lo:, cols], k, (((1,), (1,)), ((), ())), preferred_element_type=F32)
            sp = jnp.maximum(z, 0.0) + jnp.log2(1.0 + jnp.exp2(-jnp.abs(z)))
            sp_diag = jnp.where(causal, sp[:tile], 0.0)
            sp = sp_diag if first else jnp.concatenate([sp_diag, sp[tile:]], axis=0)
            suffix = jnp.dot(sp.astype(BF16), neg_later, preferred_element_type=F32)
            arg = (z - sp) + suffix
            if not first:
                carry = carry_ref[hi:, cols]
                carry = jnp.concatenate([carry] * (tile // LANES), axis=1)
                arg = jnp.concatenate([arg[:tile], arg[tile:] + carry], axis=0)
            w = jnp.exp2(arg)
            w_diag = jnp.where(causal, w[:tile], 0.0)
            w = w_diag if first else jnp.concatenate([w_diag, w[tile:]], axis=0)
            pv = jnp.dot(w.astype(BF16), v, preferred_element_type=F32)
            total = -jnp.broadcast_to(jnp.sum(sp, axis=1, keepdims=True), (sp.shape[0], LANES))
            acc_ref[lo:hi, cols] = pv[:tile]
            carry_ref[lo:hi, cols] = total[:tile]
            if not first:
                acc_ref[hi:, cols] += pv[tile:]
                carry_ref[hi:, cols] += total[tile:]
    o_ref[...] = (acc_ref[...] * g_ref[...].astype(F32)).astype(o_ref.dtype)


def _attention(q, k, v, gate, *, batch, seq):
    m, d = q.shape
    heads = 2 if d % (2 * HEAD_DIM) == 0 else 1
    width = heads * HEAD_DIM
    tile = min(256, seq)
    r3 = lambda a: a.reshape(batch, seq, d)
    spec = pl.BlockSpec((None, seq, width), lambda b, h: (b, 0, h))
    est = 10 * seq * width * 2 + 2 * seq * width * 4 + 8 * heads * seq * tile * 4
    out = pl.pallas_call(
        functools.partial(_attn_kernel, tile=tile, heads=heads),
        out_shape=jax.ShapeDtypeStruct((batch, seq, d), BF16),
        grid=(batch, d // width),
        in_specs=[spec, spec, spec, spec],
        out_specs=spec,
        scratch_shapes=[pltpu.VMEM((seq, width), F32),
                        pltpu.VMEM((seq, width), F32)],
        compiler_params=pltpu.CompilerParams(
            dimension_semantics=("parallel", "parallel"),
            vmem_limit_bytes=_vmem_limit(est + (8 << 20))),
        name="stickbreak_attn",
    )(r3(q), r3(k), r3(v), r3(gate))
    return out.reshape(m, d)


def _dot_nt_3pass(a, b):
    dims = (((1,), (1,)), ((), ()))
    a_hi = a.astype(BF16)
    b_hi = b.astype(BF16)
    a_lo = (a - a_hi.astype(F32)).astype(BF16)
    b_lo = (b - b_hi.astype(F32)).astype(BF16)
    return (lax.dot_general(a_hi, b_hi, dims, preferred_element_type=F32)
            + lax.dot_general(a_hi, b_lo, dims, preferred_element_type=F32)
            + lax.dot_general(a_lo, b_hi, dims, preferred_element_type=F32))


def _ssm_prep_kernel(p_ref, bre_ref, bim_ref, cre_ref, cim_ref, t_ref, w_ref, vt_ref, lam_ref):
    n_rows = 24
    n = lax.broadcasted_iota(jnp.int32, (n_rows, LANES), 0).astype(F32)
    lane = lax.broadcasted_iota(jnp.int32, (GROUP, LANES), 1)
    own_half = (lane < STATE, lane >= STATE)

    def per_pair(pi, _):
        a_re = p_ref[pi, 0:1, :]
        a_im = p_ref[pi, 1:2, :]
        dt = jnp.exp(p_ref[pi, 2:3, :])
        mag = jnp.exp(n * (a_re * dt))
        ang = n * (a_im * dt)
        pre = mag * jnp.cos(ang)
        pim = mag * jnp.sin(ang)
        lam_re = pre[1:2, :]
        lam_im = pim[1:2, :]
        den = a_re * a_re + a_im * a_im
        f_re = ((lam_re - 1.0) * a_re + lam_im * a_im) / den
        f_im = (lam_im * a_re - (lam_re - 1.0) * a_im) / den
        fre = f_re * pre - f_im * pim
        fim = f_re * pim + f_im * pre
        bre = bre_ref[pi]
        bim = bim_ref[pi]
        cre = cre_ref[pi]
        cim = cim_ref[pi]

        v_pair = []
        for e in range(CHUNK + 1):
            v_pair.append((cre * pre[e:e + 1, :] - cim * pim[e:e + 1, :],
                           -(cre * pim[e:e + 1, :] + cim * pre[e:e + 1, :])))
        w_pair = []
        for e in range(CHUNK):
            w_pair.append((fre[e:e + 1, :] * bre - fim[e:e + 1, :] * bim,
                           fre[e:e + 1, :] * bim + fim[e:e + 1, :] * bre))
        lam_pair = jnp.concatenate([pre[CHUNK:CHUNK + 1, :], pim[CHUNK:CHUNK + 1, :],
                                    jnp.zeros((6, LANES), F32)], axis=0)

        for parity in range(2):
            gi = 2 * pi + parity
            keep = own_half[parity]
            own = lambda ab: jnp.concatenate([jnp.where(keep, ab[0], 0.0), jnp.where(keep, ab[1], 0.0)], axis=1)
            v_blocks = [own(v) for v in v_pair]
            for t in range(CHUNK):
                vt_ref[gi, t * GROUP:(t + 1) * GROUP, :] = v_blocks[t + 1].astype(vt_ref.dtype)
            v0t = jnp.concatenate(v_blocks[:CHUNK], axis=0)
            for s in range(CHUNK):
                w_ref[gi, s * GROUP:(s + 1) * GROUP, :] = own(w_pair[CHUNK - 1 - s]).astype(w_ref.dtype)
            wb0 = own(w_pair[0])
            kmat = _dot_nt_3pass(wb0, v0t)
            k0 = kmat[:, :LANES]
            k1 = kmat[:, LANES:]
            zero = jnp.zeros_like(k0)
            for s in range(CHUNK):
                sh = s * GROUP
                if s == 0:
                    lo, hi = k0, k1
                elif sh < LANES:
                    r0 = pltpu.roll(k0, sh, 1)
                    r1 = pltpu.roll(k1, sh, 1)
                    lo = jnp.where(lane >= sh, r0, 0.0)
                    hi = jnp.where(lane >= sh, r1, r0)
                elif sh == LANES:
                    lo, hi = zero, k0
                else:
                    r0 = pltpu.roll(k0, sh - LANES, 1)
                    lo = zero
                    hi = jnp.where(lane >= sh - LANES, r0, 0.0)
                t_ref[gi, s * GROUP:(s + 1) * GROUP, :] = jnp.concatenate([lo, hi], axis=1).astype(t_ref.dtype)
            lam_ref[gi] = jnp.where(keep[:8], lam_pair, 0.0)
        return 0

    lax.fori_loop(0, p_ref.shape[0], per_pair, 0, unroll=True)


def _ssm_prep(a_re, a_im, log_dt, b_re, b_im, c_re, c_im):
    g = a_re.shape[0]

    def pair_lanes(a):
        r = a.shape[1]
        return a.reshape(g // 2, 2, r, STATE).transpose(0, 2, 1, 3).reshape(g // 2, r, LANES)

    rows = jnp.stack([a_re, a_im, jnp.broadcast_to(log_dt[:, None], (g, STATE))]
                     + [jnp.zeros((g, STATE), F32)] * 5, axis=1).astype(F32)
    params = pair_lanes(rows)
    bt_re = pair_lanes(jnp.swapaxes(b_re, 1, 2))
    bt_im = pair_lanes(jnp.swapaxes(b_im, 1, 2))
    cp_re = pair_lanes(c_re)
    cp_im = pair_lanes(c_im)
    gb = GROUPS_PER_TILE
    kk = CHUNK * GROUP
    spec8p = pl.BlockSpec((gb // 2, 8, LANES), lambda i: (i, 0, 0))
    spec16p = pl.BlockSpec((gb // 2, GROUP, LANES), lambda i: (i, 0, 0))
    spec8 = pl.BlockSpec((gb, 8, LANES), lambda i: (i, 0, 0))
    specm = pl.BlockSpec((gb, kk, kk), lambda i: (i, 0, 0))
    return pl.pallas_call(
        _ssm_prep_kernel,
        out_shape=(jax.ShapeDtypeStruct((g, kk, kk), BF16),
                   jax.ShapeDtypeStruct((g, kk, 2 * LANES), BF16),
                   jax.ShapeDtypeStruct((g, kk, 2 * LANES), BF16),
                   jax.ShapeDtypeStruct((g, 8, LANES), F32)),
        grid=(g // gb,),
        in_specs=[spec8p, spec16p, spec16p, spec16p, spec16p],
        out_specs=(specm, specm, specm, spec8),
        compiler_params=pltpu.CompilerParams(dimension_semantics=("parallel",)),
        name="ssm_prep",
    )(params, bt_re, bt_im, cp_re, cp_im)


def _ssm_kernel(u_ref, t_ref, w_ref, vt_ref, lam_ref, d_ref, o_ref, zs, ys, ss, hs, yscr, *, batch, nchunk,
                block_rows):
    n_rows = batch * nchunk
    pairs = GROUPS_PER_TILE // 2
    pitch = nchunk + SCAN_PITCH_PAD
    lane_group = lax.broadcasted_iota(jnp.int32, (block_rows, LANES), 1) // GROUP
    bit_set = {d: (lane_group & d) != 0 for d in (4, 2, 1)}

    def transpose_pieces(xs):
        for d in (4, 2, 1):
            new = list(xs)
            for m in range(GROUPS_PER_TILE):
                if m & d:
                    continue
                a, b = xs[m], xs[m + d]
                new[m] = jnp.where(bit_set[d], pltpu.roll(b, GROUP * d, 1), a)
                new[m + d] = jnp.where(bit_set[d], b, pltpu.roll(a, LANES - GROUP * d, 1))
            xs = new
        return xs

    def position_rows(r0, t):
        return pl.ds(r0 * CHUNK + t, block_rows, stride=CHUNK)

    row_starts = range(0, n_rows, block_rows)

    for r0 in row_starts:
        for half in range(2):
            pieces = transpose_pieces([u_ref[position_rows(r0, half * 8 + m), :] for m in range(GROUPS_PER_TILE)])
            for g in range(GROUPS_PER_TILE):
                zs[g, r0:r0 + block_rows, half * LANES:(half + 1) * LANES] = pieces[g].astype(zs.dtype)

    for p in range(pairs):
        s_end = (jnp.dot(zs[2 * p], w_ref[2 * p], preferred_element_type=F32)
                 + jnp.dot(zs[2 * p + 1], w_ref[2 * p + 1], preferred_element_type=F32))
        for b in range(batch):
            ss[2 * p, b * pitch:b * pitch + nchunk, :] = s_end[b * nchunk:(b + 1) * nchunk, :LANES]
            ss[2 * p + 1, b * pitch:b * pitch + nchunk, :] = s_end[b * nchunk:(b + 1) * nchunk, LANES:]

    lre = [lam_ref[2 * p, 0:1, :] + lam_ref[2 * p + 1, 0:1, :] for p in range(pairs)]
    lim = [lam_ref[2 * p, 1:2, :] + lam_ref[2 * p + 1, 1:2, :] for p in range(pairs)]

    def scan(c, carry):
        new = []
        for p in range(pairs):
            h_re, h_im = carry[2 * p], carry[2 * p + 1]
            rows = pl.ds(c, batch, stride=pitch)
            hs[2 * p, rows, :] = h_re
            hs[2 * p + 1, rows, :] = h_im
            s_re = ss[2 * p, rows, :]
            s_im = ss[2 * p + 1, rows, :]
            new.append(h_re * lre[p] - h_im * lim[p] + s_re)
            new.append(h_im * lre[p] + h_re * lim[p] + s_im)
        return tuple(new)

    zero_state = jnp.zeros((batch, LANES), F32)
    lax.fori_loop(0, nchunk, scan, tuple(zero_state for _ in range(2 * pairs)))

    def entering(q):
        return jnp.concatenate([hs[q, b * pitch:b * pitch + nchunk, :] for b in range(batch)], axis=0)

    for g in range(GROUPS_PER_TILE):
        h_in = jnp.concatenate([entering(2 * (g // 2)), entering(2 * (g // 2) + 1)], axis=1).astype(BF16)
        ys[g] = (jnp.dot(zs[g], t_ref[g], preferred_element_type=F32)
                 + lax.dot_general(h_in, vt_ref[g], (((1,), (1,)), ((), ())), preferred_element_type=F32))

    d_row = d_ref[...]
    for r0 in row_starts:
        for half in range(2):
            pieces = transpose_pieces([ys[g, r0:r0 + block_rows, half * LANES:(half + 1) * LANES]
                                       for g in range(GROUPS_PER_TILE)])
            for m in range(GROUPS_PER_TILE):
                rows = position_rows(r0, half * 8 + m)
                yscr[rows, :] = jax.nn.gelu(pieces[m] + d_row * u_ref[rows, :])

    o_ref[...] = yscr[...].astype(o_ref.dtype)


def _ssm(u, tmat, wmat, vtmat, lam, d_vec, *, batch, seq):
    m, d = u.shape
    nchunk = seq // CHUNK
    n_rows = batch * nchunk
    n_pad = batch * (nchunk + SCAN_PITCH_PAD)
    gb = GROUPS_PER_TILE
    kk = CHUNK * GROUP
    block_rows = min(n_rows, 128)
    specm = pl.BlockSpec((gb, kk, kk), lambda i: (i, 0, 0))
    est = (2 * m * LANES * (4 + 2) + 2 * 3 * gb * kk * kk * 2 + gb * n_rows * kk * (2 + 4)
           + 2 * gb * n_pad * LANES * 4 + m * LANES * 4 + 8 * n_rows * kk * 4)
    return pl.pallas_call(
        functools.partial(_ssm_kernel, batch=batch, nchunk=nchunk, block_rows=block_rows),
        out_shape=jax.ShapeDtypeStruct((m, d), BF16),
        grid=(d // LANES,),
        in_specs=[pl.BlockSpec((m, LANES), lambda i: (0, i)),
                  specm, specm, specm,
                  pl.BlockSpec((gb, 8, LANES), lambda i: (i, 0, 0)),
                  pl.BlockSpec((1, LANES), lambda i: (0, i))],
        out_specs=pl.BlockSpec((m, LANES), lambda i: (0, i)),
        scratch_shapes=[pltpu.VMEM((gb, n_rows, kk), BF16),
                        pltpu.VMEM((gb, n_rows, kk), F32),
                        pltpu.VMEM((gb, n_pad, LANES), F32),
                        pltpu.VMEM((gb, n_pad, LANES), F32),
                        pltpu.VMEM((m, LANES), F32)],
        compiler_params=pltpu.CompilerParams(
            dimension_semantics=("parallel",),
            vmem_limit_bytes=_vmem_limit(est)),
        name="s5_scan",
    )(u, tmat, wmat, vtmat, lam, d_vec.reshape(1, d))


def kernel(x, norm_g, attn_w_in, attn_q_g, attn_k_g, attn_w_out, ssm_w_in, ssm_A_re, ssm_A_im, ssm_log_dt,
           ssm_B_re, ssm_B_im, ssm_C_re, ssm_C_im, ssm_D, ssm_glu_w, ssm_glu_b, ssm_w_out):
    batch, seq, d = x.shape
    m = batch * seq
    assert d % LANES == 0 and seq % 256 == 0 and (m // CHUNK) % 8 == 0
    assert ssm_A_re.shape[1:] == (d // GROUP, STATE) and (d // GROUP) % GROUPS_PER_TILE == 0
    x2 = x.reshape(m, d)

    h = _rmsnorm(x2, norm_g[0])
    w_in = attn_w_in[0]
    qg = attn_q_g[0].reshape(1, HEAD_DIM)
    kg = attn_k_g[0].reshape(1, HEAD_DIM)
    q = _matmul(h, w_in, col_off=0, n=d, mode="headnorm", out_dtype=BF16, extras=(qg,),
                scale=LOG2E / math.sqrt(HEAD_DIM))
    k = _matmul(h, w_in, col_off=d, n=d, mode="headnorm", out_dtype=BF16, extras=(kg,))
    v = _matmul(h, w_in, col_off=2 * d, n=d, mode="store", out_dtype=BF16)
    gate = _matmul(h, w_in, col_off=3 * d, n=d, mode="silu", out_dtype=BF16)
    og = _attention(q, k, v, gate, batch=batch, seq=seq)
    x2, h, ssq = _matmul(og, attn_w_out[0], col_off=0, n=d, mode="residual", out_dtype=F32, extras=(x2,),
                         next_gain=norm_g[1])

    w_in = ssm_w_in[0]
    u = _matmul(h, w_in, col_off=0, n=d, mode="store", out_dtype=F32, row_ssq=ssq)
    gate = _matmul(h, w_in, col_off=d, n=d, mode="silu", out_dtype=BF16, row_ssq=ssq)
    tmat, wmat, vtmat, lam = _ssm_prep(ssm_A_re[0], ssm_A_im[0], ssm_log_dt[0], ssm_B_re[0], ssm_B_im[0],
                                       ssm_C_re[0], ssm_C_im[0])
    y = _ssm(u, tmat, wmat, vtmat, lam, ssm_D[0], batch=batch, seq=seq)
    yg = _matmul(y, ssm_glu_w[0], col_off=0, n=d, mode="glu", out_dtype=BF16,
                 extras=(ssm_glu_b[0].reshape(1, d), y, gate))
    x2 = _matmul(yg, ssm_w_out[0], col_off=0, n=d, mode="residual", out_dtype=F32, extras=(x2,))
    return x2.reshape(batch, seq, d)
```

```python
import functools
import math

import jax
import jax.numpy as jnp
from jax import lax
from jax.experimental import pallas as pl
from jax.experimental.pallas import tpu as pltpu

HEAD_DIM = 128
GROUP = 16
STATE = 64
RMS_EPS = 1e-6
LANES = 128
CHUNK = 16
LOG2E = 1.4426950408889634
SCAN_PITCH_PAD = 8
GROUPS_PER_TILE = LANES // GROUP
VMEM_CAP_BYTES = 60 * 1024 * 1024

F32 = jnp.float32
BF16 = jnp.bfloat16


def _vmem_limit(estimate_bytes):
    return int(min(VMEM_CAP_BYTES, max(32 * 1024 * 1024, estimate_bytes)))


def _rmsnorm_kernel(x_ref, g_ref, o_ref):
    x = x_ref[...]
    r = lax.rsqrt(jnp.mean(x * x, axis=-1, keepdims=True) + RMS_EPS)
    o_ref[...] = ((x * r) * g_ref[...]).astype(o_ref.dtype)


def _rmsnorm(x2, g):
    m, d = x2.shape
    tm = min(512, m)
    return pl.pallas_call(
        _rmsnorm_kernel,
        out_shape=jax.ShapeDtypeStruct((m, d), BF16),
        grid=(m // tm,),
        in_specs=[pl.BlockSpec((tm, d), lambda i: (i, 0)),
                  pl.BlockSpec((1, d), lambda i: (0, 0))],
        out_specs=pl.BlockSpec((tm, d), lambda i: (i, 0)),
        compiler_params=pltpu.CompilerParams(dimension_semantics=("parallel",),
                                             vmem_limit_bytes=_vmem_limit(2 * tm * d * (4 + 2) + 4 * tm * d * 4)),
        name="rmsnorm",
    )(x2, g.reshape(1, d))


def _mm_epilogue(acc, extras, outs, *, mode, scale, row_ssq, emit_norm):
    extras = list(extras)
    if row_ssq:
        ssq = extras.pop(0)[...]
        total = sum(ssq[:, c * LANES:(c + 1) * LANES] for c in range(ssq.shape[1] // LANES))
        r = lax.rsqrt(total * (1.0 / row_ssq) + RMS_EPS)
        acc = acc * jnp.concatenate([r] * (acc.shape[1] // LANES), axis=1)
    o_ref = outs[0]
    if mode == "headnorm":
        g = extras[0][...]
        for h in range(acc.shape[1] // HEAD_DIM):
            sl = slice(h * HEAD_DIM, (h + 1) * HEAD_DIM)
            a = acc[:, sl]
            r = lax.rsqrt(jnp.mean(a * a, axis=-1, keepdims=True) + RMS_EPS)
            o_ref[:, sl] = (((a * r) * g) * scale).astype(o_ref.dtype)
    elif mode == "silu":
        o_ref[...] = (acc * jax.nn.sigmoid(acc)).astype(o_ref.dtype)
    elif mode == "residual":
        y = extras[0][...] + acc
        o_ref[...] = y.astype(o_ref.dtype)
        if emit_norm:
            outs[1][...] = (y * extras[1][...]).astype(outs[1].dtype)
            outs[2][...] = jnp.broadcast_to(jnp.sum(y * y, axis=1, keepdims=True), outs[2].shape)
    elif mode == "glu":
        b_ref, y_ref, gate_ref = extras[0], extras[1], extras[2]
        t = acc + b_ref[...]
        o_ref[...] = (y_ref[...].astype(F32) * jax.nn.sigmoid(t) * gate_ref[...].astype(F32)).astype(o_ref.dtype)
    else:
        o_ref[...] = acc.astype(o_ref.dtype)


def _mm_kernel(x_ref, w_ref, *rest, mode, scale, n_out, row_ssq, emit_norm):
    acc = jnp.dot(x_ref[...], w_ref[...].astype(BF16), preferred_element_type=F32)
    _mm_epilogue(acc, rest[:-n_out], rest[-n_out:], mode=mode, scale=scale, row_ssq=row_ssq, emit_norm=emit_norm)


def _matmul(x, w, *, col_off, n, mode, out_dtype, extras=(), scale=1.0, tm=512, tn=1024, row_ssq=None,
            next_gain=None):
    m, k = x.shape
    tm = min(tm, m)
    tn = min(tn, n)
    off = col_off // tn
    col_tiles = n // tn
    tile_spec = pl.BlockSpec((tm, tn), lambda j, i: (i, j))
    in_specs = [pl.BlockSpec((tm, k), lambda j, i: (i, 0)),
                pl.BlockSpec((k, tn), lambda j, i: (0, j + off))]
    operands = [x, w]
    extra_bytes = 0
    if row_ssq is not None:
        operands.append(row_ssq)
        in_specs.append(pl.BlockSpec((tm, row_ssq.shape[1]), lambda j, i: (i, 0)))
    extras = tuple(extras) + ((next_gain.reshape(1, n),) if next_gain is not None else ())
    for e in extras:
        operands.append(e)
        if e.shape[0] == 1:
            width = e.shape[1]
            if width == n:
                in_specs.append(pl.BlockSpec((1, tn), lambda j, i: (0, j)))
            else:
                in_specs.append(pl.BlockSpec((1, width), lambda j, i: (0, 0)))
        else:
            in_specs.append(tile_spec)
            extra_bytes += tm * tn * e.dtype.itemsize
    out_shape = [jax.ShapeDtypeStruct((m, n), out_dtype)]
    out_specs = [tile_spec]
    if next_gain is not None:
        out_shape += [jax.ShapeDtypeStruct((m, n), BF16), jax.ShapeDtypeStruct((m, col_tiles * LANES), F32)]
        out_specs += [tile_spec, pl.BlockSpec((tm, LANES), lambda j, i: (i, j))]
        extra_bytes += tm * tn * 2
    out_bytes = tm * tn * jnp.dtype(out_dtype).itemsize
    est = (2 * (tm * k * 2 + k * tn * w.dtype.itemsize + out_bytes + extra_bytes)
           + k * tn * 2 + 3 * tm * tn * 4)
    outs = pl.pallas_call(
        functools.partial(_mm_kernel, mode=mode, scale=scale, n_out=len(out_shape),
                          row_ssq=(k if row_ssq is not None else None), emit_norm=next_gain is not None),
        out_shape=out_shape,
        grid=(col_tiles, m // tm),
        in_specs=in_specs,
        out_specs=out_specs,
        compiler_params=pltpu.CompilerParams(
            dimension_semantics=("parallel", "parallel"),
            vmem_limit_bytes=_vmem_limit(est + (4 << 20))),
        name="mm_" + mode,
    )(*operands)
    return outs[0] if next_gain is None else outs


def _matmul_streamed(x, w, *, col_off, n, out_dtype, tm=512, tn=1024):
    m, k = x.shape
    off = col_off // tn

    def tile(x_ref, w_ref, o_ref):
        o_ref[...] = jnp.dot(x_ref[...], w_ref[...].astype(BF16), preferred_element_type=F32).astype(o_ref.dtype)

    def call(x_hbm, w_hbm, o_hbm):
        pltpu.emit_pipeline(
            tile, grid=(n // tn, m // tm),
            in_specs=[pl.BlockSpec((tm, k), lambda j, i: (i, 0)),
                      pl.BlockSpec((k, tn), lambda j, i: (0, j + off))],
            out_specs=[pl.BlockSpec((tm, tn), lambda j, i: (i, j))],
        )(x_hbm, w_hbm, o_hbm)

    est = 2 * (tm * k * 2 + k * tn * 4 + tm * tn * 2) + k * tn * 2 + 3 * tm * tn * 4
    return pl.pallas_call(
        call,
        out_shape=jax.ShapeDtypeStruct((m, n), out_dtype),
        in_specs=[pl.BlockSpec(memory_space=pl.ANY), pl.BlockSpec(memory_space=pl.ANY)],
        out_specs=pl.BlockSpec(memory_space=pl.ANY),
        compiler_params=pltpu.CompilerParams(vmem_limit_bytes=_vmem_limit(est + (4 << 20))),
        name="mm_streamed",
    )(x, w)


def _attn_kernel(q_ref, k_ref, v_ref, g_ref, o_ref, acc_ref, carry_ref, *, tile, heads):
    s_len = q_ref.shape[0]
    nt = s_len // tile
    row = lax.broadcasted_iota(jnp.int32, (tile, tile), 0)
    col = lax.broadcasted_iota(jnp.int32, (tile, tile), 1)
    causal = col < row
    neg_later = jnp.where(row > col, -1.0, 0.0).astype(BF16)
    for j in reversed(range(nt)):
        first = j == nt - 1
        lo, hi = j * tile, (j + 1) * tile
        for h in range(heads):
            cols = slice(h * HEAD_DIM, (h + 1) * HEAD_DIM)
            k = k_ref[lo:hi, cols]
            v = v_ref[lo:hi, cols]
            z = lax.dot_general(q_ref[lo:, cols], k, (((1,), (1,)), ((), ())), preferred_element_type=F32)
            sp = jnp.maximum(z, 0.0) + jnp.log2(1.0 + jnp.exp2(-jnp.abs(z)))
            sp_diag = jnp.where(causal, sp[:tile], 0.0)
            sp = sp_diag if first else jnp.concatenate([sp_diag, sp[tile:]], axis=0)
            suffix = jnp.dot(sp.astype(BF16), neg_later, preferred_element_type=F32)
            arg = (z - sp) + suffix
            if not first:
                carry = carry_ref[hi:, cols]
                carry = jnp.concatenate([carry] * (tile // LANES), axis=1)
                arg = jnp.concatenate([arg[:tile], arg[tile:] + carry], axis=0)
            w = jnp.exp2(arg)
            w_diag = jnp.where(causal, w[:tile], 0.0)
            w = w_diag if first else jnp.concatenate([w_diag, w[tile:]], axis=0)
            pv = jnp.dot(w.astype(BF16), v, preferred_element_type=F32)
            total = -jnp.broadcast_to(jnp.sum(sp, axis=1, keepdims=True), (sp.shape[0], LANES))
            acc_ref[lo:hi, cols] = pv[:tile]
            carry_ref[lo:hi, cols] = total[:tile]
            if not first:
                acc_ref[hi:, cols] += pv[tile:]
                carry_ref[hi:, cols] += total[tile:]
    o_ref[...] = (acc_ref[...] * g_ref[...].astype(F32)).astype(o_ref.dtype)


def _attention(q, k, v, gate, *, batch, seq):
    m, d = q.shape
    heads = 2 if d % (2 * HEAD_DIM) == 0 else 1
    width = heads * HEAD_DIM
    tile = min(256, seq)
    r3 = lambda a: a.reshape(batch, seq, d)
    spec = pl.BlockSpec((None, seq, width), lambda b, h: (b, 0, h))
    est = 10 * seq * width * 2 + 2 * seq * width * 4 + 8 * heads * seq * tile * 4
    out = pl.pallas_call(
        functools.partial(_attn_kernel, tile=tile, heads=heads),
        out_shape=jax.ShapeDtypeStruct((batch, seq, d), BF16),
        grid=(batch, d // width),
        in_specs=[spec, spec, spec, spec],
        out_specs=spec,
        scratch_shapes=[pltpu.VMEM((seq, width), F32),
                        pltpu.VMEM((seq, width), F32)],
        compiler_params=pltpu.CompilerParams(
            dimension_semantics=("parallel", "parallel"),
            vmem_limit_bytes=_vmem_limit(est + (8 << 20))),
        name="stickbreak_attn",
    )(r3(q), r3(k), r3(v), r3(gate))
    return out.reshape(m, d)


def _dot_nt_3pass(a, b):
    dims = (((1,), (1,)), ((), ()))
    a_hi = a.astype(BF16)
    b_hi = b.astype(BF16)
    a_lo = (a - a_hi.astype(F32)).astype(BF16)
    b_lo = (b - b_hi.astype(F32)).astype(BF16)
    return (lax.dot_general(a_hi, b_hi, dims, preferred_element_type=F32)
            + lax.dot_general(a_hi, b_lo, dims, preferred_element_type=F32)
            + lax.dot_general(a_lo, b_hi, dims, preferred_element_type=F32))


def _ssm_prep_kernel(p_ref, bre_ref, bim_ref, cre_ref, cim_ref, t_ref, w_ref, vt_ref, lam_ref):
    n_rows = 24
    n = lax.broadcasted_iota(jnp.int32, (n_rows, LANES), 0).astype(F32)
    lane = lax.broadcasted_iota(jnp.int32, (GROUP, LANES), 1)
    own_half = (lane < STATE, lane >= STATE)

    def per_pair(pi, _):
        a_re = p_ref[pi, 0:1, :]
        a_im = p_ref[pi, 1:2, :]
        dt = jnp.exp(p_ref[pi, 2:3, :])
        mag = jnp.exp(n * (a_re * dt))
        ang = n * (a_im * dt)
        pre = mag * jnp.cos(ang)
        pim = mag * jnp.sin(ang)
        lam_re = pre[1:2, :]
        lam_im = pim[1:2, :]
        den = a_re * a_re + a_im * a_im
        f_re = ((lam_re - 1.0) * a_re + lam_im * a_im) / den
        f_im = (lam_im * a_re - (lam_re - 1.0) * a_im) / den
        fre = f_re * pre - f_im * pim
        fim = f_re * pim + f_im * pre
        bre = bre_ref[pi]
        bim = bim_ref[pi]
        cre = cre_ref[pi]
        cim = cim_ref[pi]

        v_pair = []
        for e in range(CHUNK + 1):
            v_pair.append((cre * pre[e:e + 1, :] - cim * pim[e:e + 1, :],
                           -(cre * pim[e:e + 1, :] + cim * pre[e:e + 1, :])))
        w_pair = []
        for e in range(CHUNK):
            w_pair.append((fre[e:e + 1, :] * bre - fim[e:e + 1, :] * bim,
                           fre[e:e + 1, :] * bim + fim[e:e + 1, :] * bre))
        lam_pair = jnp.concatenate([pre[CHUNK:CHUNK + 1, :], pim[CHUNK:CHUNK + 1, :],
                                    jnp.zeros((6, LANES), F32)], axis=0)

        for parity in range(2):
            gi = 2 * pi + parity
            keep = own_half[parity]
            own = lambda ab: jnp.concatenate([jnp.where(keep, ab[0], 0.0), jnp.where(keep, ab[1], 0.0)], axis=1)
            v_blocks = [own(v) for v in v_pair]
            for t in range(CHUNK):
                vt_ref[gi, t * GROUP:(t + 1) * GROUP, :] = v_blocks[t + 1].astype(vt_ref.dtype)
            v0t = jnp.concatenate(v_blocks[:CHUNK], axis=0)
            for s in range(CHUNK):
                w_ref[gi, s * GROUP:(s + 1) * GROUP, :] = own(w_pair[CHUNK - 1 - s]).astype(w_ref.dtype)
            wb0 = own(w_pair[0])
            kmat = _dot_nt_3pass(wb0, v0t)
            k0 = kmat[:, :LANES]
            k1 = kmat[:, LANES:]
            zero = jnp.zeros_like(k0)
            for s in range(CHUNK):
                sh = s * GROUP
                if s == 0:
                    lo, hi = k0, k1
                elif sh < LANES:
                    r0 = pltpu.roll(k0, sh, 1)
                    r1 = pltpu.roll(k1, sh, 1)
                    lo = jnp.where(lane >= sh, r0, 0.0)
                    hi = jnp.where(lane >= sh, r1, r0)
                elif sh == LANES:
                    lo, hi = zero, k0
                else:
                    r0 = pltpu.roll(k0, sh - LANES, 1)
                    lo = zero
                    hi = jnp.where(lane >= sh - LANES, r0, 0.0)
                t_ref[gi, s * GROUP:(s + 1) * GROUP, :] = jnp.concatenate([lo, hi], axis=1).astype(t_ref.dtype)
            lam_ref[gi] = jnp.where(keep[:8], lam_pair, 0.0)
        return 0

    lax.fori_loop(0, p_ref.shape[0], per_pair, 0, unroll=True)


def _ssm_prep(a_re, a_im, log_dt, b_re, b_im, c_re, c_im):
    g = a_re.shape[0]

    def pair_lanes(a):
        r = a.shape[1]
        return a.reshape(g // 2, 2, r, STATE).transpose(0, 2, 1, 3).reshape(g // 2, r, LANES)

    rows = jnp.stack([a_re, a_im, jnp.broadcast_to(log_dt[:, None], (g, STATE))]
                     + [jnp.zeros((g, STATE), F32)] * 5, axis=1).astype(F32)
    params = pair_lanes(rows)
    bt_re = pair_lanes(jnp.swapaxes(b_re, 1, 2))
    bt_im = pair_lanes(jnp.swapaxes(b_im, 1, 2))
    cp_re = pair_lanes(c_re)
    cp_im = pair_lanes(c_im)
    gb = GROUPS_PER_TILE
    kk = CHUNK * GROUP
    spec8p = pl.BlockSpec((gb // 2, 8, LANES), lambda i: (i, 0, 0))
    spec16p = pl.BlockSpec((gb // 2, GROUP, LANES), lambda i: (i, 0, 0))
    spec8 = pl.BlockSpec((gb, 8, LANES), lambda i: (i, 0, 0))
    specm = pl.BlockSpec((gb, kk, kk), lambda i: (i, 0, 0))
    return pl.pallas_call(
        _ssm_prep_kernel,
        out_shape=(jax.ShapeDtypeStruct((g, kk, kk), BF16),
                   jax.ShapeDtypeStruct((g, kk, 2 * LANES), BF16),
                   jax.ShapeDtypeStruct((g, kk, 2 * LANES), BF16),
                   jax.ShapeDtypeStruct((g, 8, LANES), F32)),
        grid=(g // gb,),
        in_specs=[spec8p, spec16p, spec16p, spec16p, spec16p],
        out_specs=(specm, specm, specm, spec8),
        compiler_params=pltpu.CompilerParams(dimension_semantics=("parallel",)),
        name="ssm_prep",
    )(params, bt_re, bt_im, cp_re, cp_im)


def _ssm_kernel(u_ref, t_ref, w_ref, vt_ref, lam_ref, d_ref, o_ref, zs, ys, ss, hs, yscr, *, batch, nchunk,
                block_rows):
    n_rows = batch * nchunk
    pairs = GROUPS_PER_TILE // 2
    pitch = nchunk + SCAN_PITCH_PAD
    lane_group = lax.broadcasted_iota(jnp.int32, (block_rows, LANES), 1) // GROUP
    bit_set = {d: (lane_group & d) != 0 for d in (4, 2, 1)}

    def transpose_pieces(xs):
        for d in (4, 2, 1):
            new = list(xs)
            for m in range(GROUPS_PER_TILE):
                if m & d:
                    continue
                a, b = xs[m], xs[m + d]
                new[m] = jnp.where(bit_set[d], pltpu.roll(b, GROUP * d, 1), a)
                new[m + d] = jnp.where(bit_set[d], b, pltpu.roll(a, LANES - GROUP * d, 1))
            xs = new
        return xs

    def position_rows(r0, t):
        return pl.ds(r0 * CHUNK + t, block_rows, stride=CHUNK)

    row_starts = range(0, n_rows, block_rows)

    for r0 in row_starts:
        for half in range(2):
            pieces = transpose_pieces([u_ref[position_rows(r0, half * 8 + m), :] for m in range(GROUPS_PER_TILE)])
            for g in range(GROUPS_PER_TILE):
                zs[g, r0:r0 + block_rows, half * LANES:(half + 1) * LANES] = pieces[g].astype(zs.dtype)

    for p in range(pairs):
        s_end = (jnp.dot(zs[2 * p], w_ref[2 * p], preferred_element_type=F32)
                 + jnp.dot(zs[2 * p + 1], w_ref[2 * p + 1], preferred_element_type=F32))
        for b in range(batch):
            ss[2 * p, b * pitch:b * pitch + nchunk, :] = s_end[b * nchunk:(b + 1) * nchunk, :LANES]
            ss[2 * p + 1, b * pitch:b * pitch + nchunk, :] = s_end[b * nchunk:(b + 1) * nchunk, LANES:]

    lre = [lam_ref[2 * p, 0:1, :] + lam_ref[2 * p + 1, 0:1, :] for p in range(pairs)]
    lim = [lam_ref[2 * p, 1:2, :] + lam_ref[2 * p + 1, 1:2, :] for p in range(pairs)]

    def scan(c, carry):
        new = []
        for p in range(pairs):
            h_re, h_im = carry[2 * p], carry[2 * p + 1]
            rows = pl.ds(c, batch, stride=pitch)
            hs[2 * p, rows, :] = h_re
            hs[2 * p + 1, rows, :] = h_im
            s_re = ss[2 * p, rows, :]
            s_im = ss[2 * p + 1, rows, :]
            new.append(h_re * lre[p] - h_im * lim[p] + s_re)
            new.append(h_im * lre[p] + h_re * lim[p] + s_im)
        return tuple(new)

    zero_state = jnp.zeros((batch, LANES), F32)
    lax.fori_loop(0, nchunk, scan, tuple(zero_state for _ in range(2 * pairs)))

    def entering(q):
        return jnp.concatenate([hs[q, b * pitch:b * pitch + nchunk, :] for b in range(batch)], axis=0)

    for g in range(GROUPS_PER_TILE):
        h_in = jnp.concatenate([entering(2 * (g // 2)), entering(2 * (g // 2) + 1)], axis=1).astype(BF16)
        ys[g] = (jnp.dot(zs[g], t_ref[g], preferred_element_type=F32)
                 + lax.dot_general(h_in, vt_ref[g], (((1,), (1,)), ((), ())), preferred_element_type=F32))

    d_row = d_ref[...]
    for r0 in row_starts:
        for half in range(2):
            pieces = transpose_pieces([ys[g, r0:r0 + block_rows, half * LANES:(half + 1) * LANES]
                                       for g in range(GROUPS_PER_TILE)])
            for m in range(GROUPS_PER_TILE):
                rows = position_rows(r0, half * 8 + m)
                yscr[rows, :] = jax.nn.gelu(pieces[m] + d_row * u_ref[rows, :])

    o_ref[...] = yscr[...].astype(o_ref.dtype)


def _ssm(u, tmat, wmat, vtmat, lam, d_vec, *, batch, seq):
    m, d = u.shape
    nchunk = seq // CHUNK
    n_rows = batch * nchunk
    n_pad = batch * (nchunk + SCAN_PITCH_PAD)
    gb = GROUPS_PER_TILE
    kk = CHUNK * GROUP
    block_rows = min(n_rows, 128)
    specm = pl.BlockSpec((gb, kk, kk), lambda i: (i, 0, 0))
    est = (2 * m * LANES * (4 + 2) + 2 * 3 * gb * kk * kk * 2 + gb * n_rows * kk * (2 + 4)
           + 2 * gb * n_pad * LANES * 4 + m * LANES * 4 + 8 * n_rows * kk * 4)
    return pl.pallas_call(
        functools.partial(_ssm_kernel, batch=batch, nchunk=nchunk, block_rows=block_rows),
        out_shape=jax.ShapeDtypeStruct((m, d), BF16),
        grid=(d // LANES,),
        in_specs=[pl.BlockSpec((m, LANES), lambda i: (0, i)),
                  specm, specm, specm,
                  pl.BlockSpec((gb, 8, LANES), lambda i: (i, 0, 0)),
                  pl.BlockSpec((1, LANES), lambda i: (0, i))],
        out_specs=pl.BlockSpec((m, LANES), lambda i: (0, i)),
        scratch_shapes=[pltpu.VMEM((gb, n_rows, kk), BF16),
                        pltpu.VMEM((gb, n_rows, kk), F32),
                        pltpu.VMEM((gb, n_pad, LANES), F32),
                        pltpu.VMEM((gb, n_pad, LANES), F32),
                        pltpu.VMEM((m, LANES), F32)],
        compiler_params=pltpu.CompilerParams(
            dimension_semantics=("parallel",),
            vmem_limit_bytes=_vmem_limit(est)),
        name="s5_scan",
    )(u, tmat, wmat, vtmat, lam, d_vec.reshape(1, d))


def kernel(x, norm_g, attn_w_in, attn_q_g, attn_k_g, attn_w_out, ssm_w_in, ssm_A_re, ssm_A_im, ssm_log_dt,
           ssm_B_re, ssm_B_im, ssm_C_re, ssm_C_im, ssm_D, ssm_glu_w, ssm_glu_b, ssm_w_out):
    batch, seq, d = x.shape
    m = batch * seq
    assert d % LANES == 0 and seq % 256 == 0 and (m // CHUNK) % 8 == 0
    assert ssm_A_re.shape[1:] == (d // GROUP, STATE) and (d // GROUP) % GROUPS_PER_TILE == 0
    x2 = x.reshape(m, d)

    h = _rmsnorm(x2, norm_g[0])
    w_in = attn_w_in[0]
    qg = attn_q_g[0].reshape(1, HEAD_DIM)
    kg = attn_k_g[0].reshape(1, HEAD_DIM)
    q = _matmul(h, w_in, col_off=0, n=d, mode="headnorm", out_dtype=BF16, extras=(qg,),
                scale=LOG2E / math.sqrt(HEAD_DIM))
    k = _matmul(h, w_in, col_off=d, n=d, mode="headnorm", out_dtype=BF16, extras=(kg,))
    v = _matmul_streamed(h, w_in, col_off=2 * d, n=d, out_dtype=BF16)
    gate = _matmul(h, w_in, col_off=3 * d, n=d, mode="silu", out_dtype=BF16)
    og = _attention(q, k, v, gate, batch=batch, seq=seq)
    x2, h, ssq = _matmul(og, attn_w_out[0], col_off=0, n=d, mode="residual", out_dtype=F32, extras=(x2,),
                         next_gain=norm_g[1])

    w_in = ssm_w_in[0]
    u = _matmul(h, w_in, col_off=0, n=d, mode="store", out_dtype=F32, row_ssq=ssq)
    gate = _matmul(h, w_in, col_off=d, n=d, mode="silu", out_dtype=BF16, row_ssq=ssq)
    tmat, wmat, vtmat, lam = _ssm_prep(ssm_A_re[0], ssm_A_im[0], ssm_log_dt[0], ssm_B_re[0], ssm_B_im[0],
                                       ssm_C_re[0], ssm_C_im[0])
    y = _ssm(u, tmat, wmat, vtmat, lam, ssm_D[0], batch=batch, seq=seq)
    yg = _matmul(y, ssm_glu_w[0], col_off=0, n=d, mode="glu", out_dtype=BF16,
                 extras=(ssm_glu_b[0].reshape(1, d), y, gate))
    x2 = _matmul(yg, ssm_w_out[0], col_off=0, n=d, mode="residual", out_dtype=F32, extras=(x2,))
    return x2.reshape(batch, seq, d)
```

```python
import functools
import math

import jax
import jax.numpy as jnp
from jax import lax
from jax.experimental import pallas as pl
from jax.experimental.pallas import tpu as pltpu

HEAD_DIM = 128
GROUP = 16
STATE = 64
RMS_EPS = 1e-6
LANES = 128
CHUNK = 16
LOG2E = 1.4426950408889634
SCAN_PITCH_PAD = 8
GROUPS_PER_TILE = LANES // GROUP
VMEM_CAP_BYTES = 60 * 1024 * 1024

F32 = jnp.float32
BF16 = jnp.bfloat16


def _vmem_limit(estimate_bytes):
    return int(min(VMEM_CAP_BYTES, max(32 * 1024 * 1024, estimate_bytes)))


def _rmsnorm_kernel(x_ref, g_ref, o_ref):
    x = x_ref[...]
    r = lax.rsqrt(jnp.mean(x * x, axis=-1, keepdims=True) + RMS_EPS)
    o_ref[...] = ((x * r) * g_ref[...]).astype(o_ref.dtype)


def _rmsnorm(x2, g):
    m, d = x2.shape
    tm = min(512, m)
    return pl.pallas_call(
        _rmsnorm_kernel,
        out_shape=jax.ShapeDtypeStruct((m, d), BF16),
        grid=(m // tm,),
        in_specs=[pl.BlockSpec((tm, d), lambda i: (i, 0)),
                  pl.BlockSpec((1, d), lambda i: (0, 0))],
        out_specs=pl.BlockSpec((tm, d), lambda i: (i, 0)),
        compiler_params=pltpu.CompilerParams(dimension_semantics=("parallel",),
                                             vmem_limit_bytes=_vmem_limit(2 * tm * d * (4 + 2) + 4 * tm * d * 4)),
        name="rmsnorm",
    )(x2, g.reshape(1, d))


def _mm_epilogue(acc, extras, outs, *, mode, scale, row_ssq, emit_norm):
    extras = list(extras)
    if row_ssq:
        ssq = extras.pop(0)[...]
        total = sum(ssq[:, c * LANES:(c + 1) * LANES] for c in range(ssq.shape[1] // LANES))
        r = lax.rsqrt(total * (1.0 / row_ssq) + RMS_EPS)
        acc = acc * jnp.concatenate([r] * (acc.shape[1] // LANES), axis=1)
    o_ref = outs[0]
    if mode == "headnorm":
        g = extras[0][...]
        for h in range(acc.shape[1] // HEAD_DIM):
            sl = slice(h * HEAD_DIM, (h + 1) * HEAD_DIM)
            a = acc[:, sl]
            r = lax.rsqrt(jnp.mean(a * a, axis=-1, keepdims=True) + RMS_EPS)
            o_ref[:, sl] = (((a * r) * g) * scale).astype(o_ref.dtype)
    elif mode == "silu":
        o_ref[...] = (acc * jax.nn.sigmoid(acc)).astype(o_ref.dtype)
    elif mode == "residual":
        y = extras[0][...] + acc
        o_ref[...] = y.astype(o_ref.dtype)
        if emit_norm:
            outs[1][...] = (y * extras[1][...]).astype(outs[1].dtype)
            outs[2][...] = jnp.broadcast_to(jnp.sum(y * y, axis=1, keepdims=True), outs[2].shape)
    elif mode == "glu":
        b_ref, y_ref, gate_ref = extras[0], extras[1], extras[2]
        t = acc + b_ref[...]
        o_ref[...] = (y_ref[...].astype(F32) * jax.nn.sigmoid(t) * gate_ref[...].astype(F32)).astype(o_ref.dtype)
    else:
        o_ref[...] = acc.astype(o_ref.dtype)


def _mm_kernel(x_ref, w_ref, *rest, mode, scale, n_out, row_ssq, emit_norm):
    acc = jnp.dot(x_ref[...], w_ref[...].astype(BF16), preferred_element_type=F32)
    _mm_epilogue(acc, rest[:-n_out], rest[-n_out:], mode=mode, scale=scale, row_ssq=row_ssq, emit_norm=emit_norm)


def _matmul(x, w, *, col_off, n, mode, out_dtype, extras=(), scale=1.0, tm=512, tn=1024, row_ssq=None,
            next_gain=None):
    m, k = x.shape
    tm = min(tm, m)
    tn = min(tn, n)
    off = col_off // tn
    col_tiles = n // tn
    tile_spec = pl.BlockSpec((tm, tn), lambda j, i: (i, j))
    in_specs = [pl.BlockSpec((tm, k), lambda j, i: (i, 0)),
                pl.BlockSpec((k, tn), lambda j, i: (0, j + off))]
    operands = [x, w]
    extra_bytes = 0
    if row_ssq is not None:
        operands.append(row_ssq)
        in_specs.append(pl.BlockSpec((tm, row_ssq.shape[1]), lambda j, i: (i, 0)))
    extras = tuple(extras) + ((next_gain.reshape(1, n),) if next_gain is not None else ())
    for e in extras:
        operands.append(e)
        if e.shape[0] == 1:
            width = e.shape[1]
            if width == n:
                in_specs.append(pl.BlockSpec((1, tn), lambda j, i: (0, j)))
            else:
                in_specs.append(pl.BlockSpec((1, width), lambda j, i: (0, 0)))
        else:
            in_specs.append(tile_spec)
            extra_bytes += tm * tn * e.dtype.itemsize
    out_shape = [jax.ShapeDtypeStruct((m, n), out_dtype)]
    out_specs = [tile_spec]
    if next_gain is not None:
        out_shape += [jax.ShapeDtypeStruct((m, n), BF16), jax.ShapeDtypeStruct((m, col_tiles * LANES), F32)]
        out_specs += [tile_spec, pl.BlockSpec((tm, LANES), lambda j, i: (i, j))]
        extra_bytes += tm * tn * 2
    out_bytes = tm * tn * jnp.dtype(out_dtype).itemsize
    est = (2 * (tm * k * 2 + k * tn * w.dtype.itemsize + out_bytes + extra_bytes)
           + k * tn * 2 + 3 * tm * tn * 4)
    outs = pl.pallas_call(
        functools.partial(_mm_kernel, mode=mode, scale=scale, n_out=len(out_shape),
                          row_ssq=(k if row_ssq is not None else None), emit_norm=next_gain is not None),
        out_shape=out_shape,
        grid=(col_tiles, m // tm),
        in_specs=in_specs,
        out_specs=out_specs,
        compiler_params=pltpu.CompilerParams(
            dimension_semantics=("parallel", "parallel"),
            vmem_limit_bytes=_vmem_limit(est + (4 << 20))),
        name="mm_" + mode,
    )(*operands)
    return outs[0] if next_gain is None else outs


def _matmul_streamed(x, w, *, col_off, n, out_dtype, mode="store", x_buffers=2, tm=512, tn=1024):
    m, k = x.shape
    off = col_off // tn

    def tile(x_ref, w_ref, o_ref):
        acc = jnp.dot(x_ref[...], w_ref[...].astype(BF16), preferred_element_type=F32)
        if mode == "silu":
            acc = acc * jax.nn.sigmoid(acc)
        o_ref[...] = acc.astype(o_ref.dtype)

    def call(x_hbm, w_hbm, o_hbm):
        pltpu.emit_pipeline(
            tile, grid=(n // tn, m // tm),
            in_specs=[pl.BlockSpec((tm, k), lambda j, i: (i, 0), pipeline_mode=pl.Buffered(x_buffers)),
                      pl.BlockSpec((k, tn), lambda j, i: (0, j + off),
                                   pipeline_mode=pl.Buffered(2, use_lookahead=True))],
            out_specs=[pl.BlockSpec((tm, tn), lambda j, i: (i, j))],
        )(x_hbm, w_hbm, o_hbm)

    est = (x_buffers * tm * k * 2 + 2 * (k * tn * 4 + tm * tn * 2)) + k * tn * 2 + 3 * tm * tn * 4
    return pl.pallas_call(
        call,
        out_shape=jax.ShapeDtypeStruct((m, n), out_dtype),
        in_specs=[pl.BlockSpec(memory_space=pl.ANY), pl.BlockSpec(memory_space=pl.ANY)],
        out_specs=pl.BlockSpec(memory_space=pl.ANY),
        compiler_params=pltpu.CompilerParams(vmem_limit_bytes=_vmem_limit(est + (4 << 20))),
        name="mm_streamed_" + mode,
    )(x, w)


def _attn_kernel(q_ref, k_ref, v_ref, g_ref, o_ref, acc_ref, carry_ref, *, tile, heads):
    s_len = q_ref.shape[0]
    nt = s_len // tile
    row = lax.broadcasted_iota(jnp.int32, (tile, tile), 0)
    col = lax.broadcasted_iota(jnp.int32, (tile, tile), 1)
    causal = col < row
    neg_later = jnp.where(row > col, -1.0, 0.0).astype(BF16)
    for j in reversed(range(nt)):
        first = j == nt - 1
        lo, hi = j * tile, (j + 1) * tile
        for h in range(heads):
            cols = slice(h * HEAD_DIM, (h + 1) * HEAD_DIM)
            k = k_ref[lo:hi, cols]
            v = v_ref[lo:hi, cols]
            z = lax.dot_general(q_ref[lo:, cols], k, (((1,), (1,)), ((), ())), preferred_element_type=F32)
            sp = jnp.maximum(z, 0.0) + jnp.log2(1.0 + jnp.exp2(-jnp.abs(z)))
            sp_diag = jnp.where(causal, sp[:tile], 0.0)
            sp = sp_diag if first else jnp.concatenate([sp_diag, sp[tile:]], axis=0)
            suffix = jnp.dot(sp.astype(BF16), neg_later, preferred_element_type=F32)
            arg = (z - sp) + suffix
            if not first:
                carry = carry_ref[hi:, cols]
                carry = jnp.concatenate([carry] * (tile // LANES), axis=1)
                arg = jnp.concatenate([arg[:tile], arg[tile:] + carry], axis=0)
            w = jnp.exp2(arg)
            w_diag = jnp.where(causal, w[:tile], 0.0)
            w = w_diag if first else jnp.concatenate([w_diag, w[tile:]], axis=0)
            pv = jnp.dot(w.astype(BF16), v, preferred_element_type=F32)
            total = -jnp.broadcast_to(jnp.sum(sp, axis=1, keepdims=True), (sp.shape[0], LANES))
            acc_ref[lo:hi, cols] = pv[:tile]
            carry_ref[lo:hi, cols] = total[:tile]
            if not first:
                acc_ref[hi:, cols] += pv[tile:]
                carry_ref[hi:, cols] += total[tile:]
    o_ref[...] = (acc_ref[...] * g_ref[...].astype(F32)).astype(o_ref.dtype)


def _attention(q, k, v, gate, *, batch, seq):
    m, d = q.shape
    heads = 2 if d % (2 * HEAD_DIM) == 0 else 1
    width = heads * HEAD_DIM
    tile = min(256, seq)
    r3 = lambda a: a.reshape(batch, seq, d)
    spec = pl.BlockSpec((None, seq, width), lambda b, h: (b, 0, h))
    est = 10 * seq * width * 2 + 2 * seq * width * 4 + 8 * heads * seq * tile * 4
    out = pl.pallas_call(
        functools.partial(_attn_kernel, tile=tile, heads=heads),
        out_shape=jax.ShapeDtypeStruct((batch, seq, d), BF16),
        grid=(batch, d // width),
        in_specs=[spec, spec, spec, spec],
        out_specs=spec,
        scratch_shapes=[pltpu.VMEM((seq, width), F32),
                        pltpu.VMEM((seq, width), F32)],
        compiler_params=pltpu.CompilerParams(
            dimension_semantics=("parallel", "parallel"),
            vmem_limit_bytes=_vmem_limit(est + (8 << 20))),
        name="stickbreak_attn",
    )(r3(q), r3(k), r3(v), r3(gate))
    return out.reshape(m, d)


def _dot_nt_3pass(a, b):
    dims = (((1,), (1,)), ((), ()))
    a_hi = a.astype(BF16)
    b_hi = b.astype(BF16)
    a_lo = (a - a_hi.astype(F32)).astype(BF16)
    b_lo = (b - b_hi.astype(F32)).astype(BF16)
    return (lax.dot_general(a_hi, b_hi, dims, preferred_element_type=F32)
            + lax.dot_general(a_hi, b_lo, dims, preferred_element_type=F32)
            + lax.dot_general(a_lo, b_hi, dims, preferred_element_type=F32))


def _ssm_prep_kernel(p_ref, bre_ref, bim_ref, cre_ref, cim_ref, t_ref, w_ref, vt_ref, lam_ref):
    n_rows = 24
    n = lax.broadcasted_iota(jnp.int32, (n_rows, LANES), 0).astype(F32)
    lane = lax.broadcasted_iota(jnp.int32, (GROUP, LANES), 1)
    own_half = (lane < STATE, lane >= STATE)

    def per_pair(pi, _):
        a_re = p_ref[pi, 0:1, :]
        a_im = p_ref[pi, 1:2, :]
        dt = jnp.exp(p_ref[pi, 2:3, :])
        mag = jnp.exp(n * (a_re * dt))
        ang = n * (a_im * dt)
        pre = mag * jnp.cos(ang)
        pim = mag * jnp.sin(ang)
        lam_re = pre[1:2, :]
        lam_im = pim[1:2, :]
        den = a_re * a_re + a_im * a_im
        f_re = ((lam_re - 1.0) * a_re + lam_im * a_im) / den
        f_im = (lam_im * a_re - (lam_re - 1.0) * a_im) / den
        fre = f_re * pre - f_im * pim
        fim = f_re * pim + f_im * pre
        bre = bre_ref[pi]
        bim = bim_ref[pi]
        cre = cre_ref[pi]
        cim = cim_ref[pi]

        v_pair = []
        for e in range(CHUNK + 1):
            v_pair.append((cre * pre[e:e + 1, :] - cim * pim[e:e + 1, :],
                           -(cre * pim[e:e + 1, :] + cim * pre[e:e + 1, :])))
        w_pair = []
        for e in range(CHUNK):
            w_pair.append((fre[e:e + 1, :] * bre - fim[e:e + 1, :] * bim,
                           fre[e:e + 1, :] * bim + fim[e:e + 1, :] * bre))
        lam_pair = jnp.concatenate([pre[CHUNK:CHUNK + 1, :], pim[CHUNK:CHUNK + 1, :],
                                    jnp.zeros((6, LANES), F32)], axis=0)

        for parity in range(2):
            gi = 2 * pi + parity
            keep = own_half[parity]
            own = lambda ab: jnp.concatenate([jnp.where(keep, ab[0], 0.0), jnp.where(keep, ab[1], 0.0)], axis=1)
            v_blocks = [own(v) for v in v_pair]
            for t in range(CHUNK):
                vt_ref[gi, t * GROUP:(t + 1) * GROUP, :] = v_blocks[t + 1].astype(vt_ref.dtype)
            v0t = jnp.concatenate(v_blocks[:CHUNK], axis=0)
            for s in range(CHUNK):
                w_ref[gi, s * GROUP:(s + 1) * GROUP, :] = own(w_pair[CHUNK - 1 - s]).astype(w_ref.dtype)
            wb0 = own(w_pair[0])
            kmat = _dot_nt_3pass(wb0, v0t)
            k0 = kmat[:, :LANES]
            k1 = kmat[:, LANES:]
            zero = jnp.zeros_like(k0)
            for s in range(CHUNK):
                sh = s * GROUP
                if s == 0:
                    lo, hi = k0, k1
                elif sh < LANES:
                    r0 = pltpu.roll(k0, sh, 1)
                    r1 = pltpu.roll(k1, sh, 1)
                    lo = jnp.where(lane >= sh, r0, 0.0)
                    hi = jnp.where(lane >= sh, r1, r0)
                elif sh == LANES:
                    lo, hi = zero, k0
                else:
                    r0 = pltpu.roll(k0, sh - LANES, 1)
                    lo = zero
                    hi = jnp.where(lane >= sh - LANES, r0, 0.0)
                t_ref[gi, s * GROUP:(s + 1) * GROUP, :] = jnp.concatenate([lo, hi], axis=1).astype(t_ref.dtype)
            lam_ref[gi] = jnp.where(keep[:8], lam_pair, 0.0)
        return 0

    lax.fori_loop(0, p_ref.shape[0], per_pair, 0, unroll=True)


def _ssm_prep(a_re, a_im, log_dt, b_re, b_im, c_re, c_im):
    g = a_re.shape[0]

    def pair_lanes(a):
        r = a.shape[1]
        return a.reshape(g // 2, 2, r, STATE).transpose(0, 2, 1, 3).reshape(g // 2, r, LANES)

    rows = jnp.stack([a_re, a_im, jnp.broadcast_to(log_dt[:, None], (g, STATE))]
                     + [jnp.zeros((g, STATE), F32)] * 5, axis=1).astype(F32)
    params = pair_lanes(rows)
    bt_re = pair_lanes(jnp.swapaxes(b_re, 1, 2))
    bt_im = pair_lanes(jnp.swapaxes(b_im, 1, 2))
    cp_re = pair_lanes(c_re)
    cp_im = pair_lanes(c_im)
    gb = GROUPS_PER_TILE
    kk = CHUNK * GROUP
    spec8p = pl.BlockSpec((gb // 2, 8, LANES), lambda i: (i, 0, 0))
    spec16p = pl.BlockSpec((gb // 2, GROUP, LANES), lambda i: (i, 0, 0))
    spec8 = pl.BlockSpec((gb, 8, LANES), lambda i: (i, 0, 0))
    specm = pl.BlockSpec((gb, kk, kk), lambda i: (i, 0, 0))
    return pl.pallas_call(
        _ssm_prep_kernel,
        out_shape=(jax.ShapeDtypeStruct((g, kk, kk), BF16),
                   jax.ShapeDtypeStruct((g, kk, 2 * LANES), BF16),
                   jax.ShapeDtypeStruct((g, kk, 2 * LANES), BF16),
                   jax.ShapeDtypeStruct((g, 8, LANES), F32)),
        grid=(g // gb,),
        in_specs=[spec8p, spec16p, spec16p, spec16p, spec16p],
        out_specs=(specm, specm, specm, spec8),
        compiler_params=pltpu.CompilerParams(dimension_semantics=("parallel",)),
        name="ssm_prep",
    )(params, bt_re, bt_im, cp_re, cp_im)


def _ssm_kernel(u_ref, t_ref, w_ref, vt_ref, lam_ref, d_ref, o_ref, zs, ys, ss, hs, yscr, *, batch, nchunk,
                block_rows):
    n_rows = batch * nchunk
    pairs = GROUPS_PER_TILE // 2
    pitch = nchunk + SCAN_PITCH_PAD
    lane_group = lax.broadcasted_iota(jnp.int32, (block_rows, LANES), 1) // GROUP
    bit_set = {d: (lane_group & d) != 0 for d in (4, 2, 1)}

    def transpose_pieces(xs):
        for d in (4, 2, 1):
            new = list(xs)
            for m in range(GROUPS_PER_TILE):
                if m & d:
                    continue
                a, b = xs[m], xs[m + d]
                new[m] = jnp.where(bit_set[d], pltpu.roll(b, GROUP * d, 1), a)
                new[m + d] = jnp.where(bit_set[d], b, pltpu.roll(a, LANES - GROUP * d, 1))
            xs = new
        return xs

    def position_rows(r0, t):
        return pl.ds(r0 * CHUNK + t, block_rows, stride=CHUNK)

    row_starts = range(0, n_rows, block_rows)

    for r0 in row_starts:
        for half in range(2):
            pieces = transpose_pieces([u_ref[position_rows(r0, half * 8 + m), :] for m in range(GROUPS_PER_TILE)])
            for g in range(GROUPS_PER_TILE):
                zs[g, r0:r0 + block_rows, half * LANES:(half + 1) * LANES] = pieces[g].astype(zs.dtype)

    for p in range(pairs):
        s_end = (jnp.dot(zs[2 * p], w_ref[2 * p], preferred_element_type=F32)
                 + jnp.dot(zs[2 * p + 1], w_ref[2 * p + 1], preferred_element_type=F32))
        for b in range(batch):
            ss[2 * p, b * pitch:b * pitch + nchunk, :] = s_end[b * nchunk:(b + 1) * nchunk, :LANES]
            ss[2 * p + 1, b * pitch:b * pitch + nchunk, :] = s_end[b * nchunk:(b + 1) * nchunk, LANES:]

    lre = [lam_ref[2 * p, 0:1, :] + lam_ref[2 * p + 1, 0:1, :] for p in range(pairs)]
    lim = [lam_ref[2 * p, 1:2, :] + lam_ref[2 * p + 1, 1:2, :] for p in range(pairs)]

    def scan(c, carry):
        new = []
        for p in range(pairs):
            h_re, h_im = carry[2 * p], carry[2 * p + 1]
            rows = pl.ds(c, batch, stride=pitch)
            hs[2 * p, rows, :] = h_re
            hs[2 * p + 1, rows, :] = h_im
            s_re = ss[2 * p, rows, :]
            s_im = ss[2 * p + 1, rows, :]
            new.append(h_re * lre[p] - h_im * lim[p] + s_re)
            new.append(h_im * lre[p] + h_re * lim[p] + s_im)
        return tuple(new)

    zero_state = jnp.zeros((batch, LANES), F32)
    lax.fori_loop(0, nchunk, scan, tuple(zero_state for _ in range(2 * pairs)))

    def entering(q):
        return jnp.concatenate([hs[q, b * pitch:b * pitch + nchunk, :] for b in range(batch)], axis=0)

    for g in range(GROUPS_PER_TILE):
        h_in = jnp.concatenate([entering(2 * (g // 2)), entering(2 * (g // 2) + 1)], axis=1).astype(BF16)
        ys[g] = (jnp.dot(zs[g], t_ref[g], preferred_element_type=F32)
                 + lax.dot_general(h_in, vt_ref[g], (((1,), (1,)), ((), ())), preferred_element_type=F32))

    d_row = d_ref[...]
    for r0 in row_starts:
        for half in range(2):
            pieces = transpose_pieces([ys[g, r0:r0 + block_rows, half * LANES:(half + 1) * LANES]
                                       for g in range(GROUPS_PER_TILE)])
            for m in range(GROUPS_PER_TILE):
                rows = position_rows(r0, half * 8 + m)
                yscr[rows, :] = jax.nn.gelu(pieces[m] + d_row * u_ref[rows, :])

    o_ref[...] = yscr[...].astype(o_ref.dtype)


def _ssm(u, tmat, wmat, vtmat, lam, d_vec, *, batch, seq):
    m, d = u.shape
    nchunk = seq // CHUNK
    n_rows = batch * nchunk
    n_pad = batch * (nchunk + SCAN_PITCH_PAD)
    gb = GROUPS_PER_TILE
    kk = CHUNK * GROUP
    block_rows = min(n_rows, 128)
    specm = pl.BlockSpec((gb, kk, kk), lambda i: (i, 0, 0))
    est = (2 * m * LANES * (4 + 2) + 2 * 3 * gb * kk * kk * 2 + gb * n_rows * kk * (2 + 4)
           + 2 * gb * n_pad * LANES * 4 + m * LANES * 4 + 8 * n_rows * kk * 4)
    return pl.pallas_call(
        functools.partial(_ssm_kernel, batch=batch, nchunk=nchunk, block_rows=block_rows),
        out_shape=jax.ShapeDtypeStruct((m, d), BF16),
        grid=(d // LANES,),
        in_specs=[pl.BlockSpec((m, LANES), lambda i: (0, i)),
                  specm, specm, specm,
                  pl.BlockSpec((gb, 8, LANES), lambda i: (i, 0, 0)),
                  pl.BlockSpec((1, LANES), lambda i: (0, i))],
        out_specs=pl.BlockSpec((m, LANES), lambda i: (0, i)),
        scratch_shapes=[pltpu.VMEM((gb, n_rows, kk), BF16),
                        pltpu.VMEM((gb, n_rows, kk), F32),
                        pltpu.VMEM((gb, n_pad, LANES), F32),
                        pltpu.VMEM((gb, n_pad, LANES), F32),
                        pltpu.VMEM((m, LANES), F32)],
        compiler_params=pltpu.CompilerParams(
            dimension_semantics=("parallel",),
            vmem_limit_bytes=_vmem_limit(est)),
        name="s5_scan",
    )(u, tmat, wmat, vtmat, lam, d_vec.reshape(1, d))


def kernel(x, norm_g, attn_w_in, attn_q_g, attn_k_g, attn_w_out, ssm_w_in, ssm_A_re, ssm_A_im, ssm_log_dt,
           ssm_B_re, ssm_B_im, ssm_C_re, ssm_C_im, ssm_D, ssm_glu_w, ssm_glu_b, ssm_w_out):
    batch, seq, d = x.shape
    m = batch * seq
    assert d % LANES == 0 and seq % 256 == 0 and (m // CHUNK) % 8 == 0
    assert ssm_A_re.shape[1:] == (d // GROUP, STATE) and (d // GROUP) % GROUPS_PER_TILE == 0
    x2 = x.reshape(m, d)

    h = _rmsnorm(x2, norm_g[0])
    w_in = attn_w_in[0]
    qg = attn_q_g[0].reshape(1, HEAD_DIM)
    kg = attn_k_g[0].reshape(1, HEAD_DIM)
    q = _matmul(h, w_in, col_off=0, n=d, mode="headnorm", out_dtype=BF16, extras=(qg,),
                scale=LOG2E / math.sqrt(HEAD_DIM))
    k = _matmul(h, w_in, col_off=d, n=d, mode="headnorm", out_dtype=BF16, extras=(kg,))
    v = _matmul_streamed(h, w_in, col_off=2 * d, n=d, out_dtype=BF16)
    gate = _matmul_streamed(h, w_in, col_off=3 * d, n=d, out_dtype=BF16, mode="silu", x_buffers=3)
    og = _attention(q, k, v, gate, batch=batch, seq=seq)
    x2, h, ssq = _matmul(og, attn_w_out[0], col_off=0, n=d, mode="residual", out_dtype=F32, extras=(x2,),
                         next_gain=norm_g[1])

    w_in = ssm_w_in[0]
    u = _matmul(h, w_in, col_off=0, n=d, mode="store", out_dtype=F32, row_ssq=ssq)
    gate = _matmul(h, w_in, col_off=d, n=d, mode="silu", out_dtype=BF16, row_ssq=ssq)
    tmat, wmat, vtmat, lam = _ssm_prep(ssm_A_re[0], ssm_A_im[0], ssm_log_dt[0], ssm_B_re[0], ssm_B_im[0],
                                       ssm_C_re[0], ssm_C_im[0])
    y = _ssm(u, tmat, wmat, vtmat, lam, ssm_D[0], batch=batch, seq=seq)
    yg = _matmul(y, ssm_glu_w[0], col_off=0, n=d, mode="glu", out_dtype=BF16,
                 extras=(ssm_glu_b[0].reshape(1, d), y, gate))
    x2 = _matmul(yg, ssm_w_out[0], col_off=0, n=d, mode="residual", out_dtype=F32, extras=(x2,))
    return x2.reshape(batch, seq, d)
```

```python
import functools
import math

import jax
import jax.numpy as jnp
from jax import lax
from jax.experimental import pallas as pl
from jax.experimental.pallas import tpu as pltpu

HEAD_DIM = 128
GROUP = 16
STATE = 64
RMS_EPS = 1e-6
LANES = 128
CHUNK = 16
LOG2E = 1.4426950408889634
SCAN_PITCH_PAD = 8
GROUPS_PER_TILE = LANES // GROUP
VMEM_CAP_BYTES = 60 * 1024 * 1024

F32 = jnp.float32
BF16 = jnp.bfloat16


def _vmem_limit(estimate_bytes):
    return int(min(VMEM_CAP_BYTES, max(32 * 1024 * 1024, estimate_bytes)))


def _rmsnorm_kernel(x_ref, g_ref, o_ref):
    x = x_ref[...]
    r = lax.rsqrt(jnp.mean(x * x, axis=-1, keepdims=True) + RMS_EPS)
    o_ref[...] = ((x * r) * g_ref[...]).astype(o_ref.dtype)


def _rmsnorm(x2, g):
    m, d = x2.shape
    tm = min(512, m)
    return pl.pallas_call(
        _rmsnorm_kernel,
        out_shape=jax.ShapeDtypeStruct((m, d), BF16),
        grid=(m // tm,),
        in_specs=[pl.BlockSpec((tm, d), lambda i: (i, 0)),
                  pl.BlockSpec((1, d), lambda i: (0, 0))],
        out_specs=pl.BlockSpec((tm, d), lambda i: (i, 0)),
        compiler_params=pltpu.CompilerParams(dimension_semantics=("parallel",),
                                             vmem_limit_bytes=_vmem_limit(2 * tm * d * (4 + 2) + 4 * tm * d * 4)),
        name="rmsnorm",
    )(x2, g.reshape(1, d))


def _mm_epilogue(acc, extras, outs, *, mode, scale, row_ssq, emit_norm):
    extras = list(extras)
    if row_ssq:
        ssq = extras.pop(0)[...]
        total = sum(ssq[:, c * LANES:(c + 1) * LANES] for c in range(ssq.shape[1] // LANES))
        r = lax.rsqrt(total * (1.0 / row_ssq) + RMS_EPS)
        acc = acc * jnp.concatenate([r] * (acc.shape[1] // LANES), axis=1)
    o_ref = outs[0]
    if mode == "headnorm":
        g = extras[0][...]
        for h in range(acc.shape[1] // HEAD_DIM):
            sl = slice(h * HEAD_DIM, (h + 1) * HEAD_DIM)
            a = acc[:, sl]
            r = lax.rsqrt(jnp.mean(a * a, axis=-1, keepdims=True) + RMS_EPS)
            o_ref[:, sl] = (((a * r) * g) * scale).astype(o_ref.dtype)
    elif mode == "silu":
        o_ref[...] = (acc * jax.nn.sigmoid(acc)).astype(o_ref.dtype)
    elif mode == "residual":
        y = extras[0][...] + acc
        o_ref[...] = y.astype(o_ref.dtype)
        if emit_norm:
            outs[1][...] = (y * extras[1][...]).astype(outs[1].dtype)
            outs[2][...] = jnp.broadcast_to(jnp.sum(y * y, axis=1, keepdims=True), outs[2].shape)
    elif mode == "glu":
        b_ref, y_ref, gate_ref = extras[0], extras[1], extras[2]
        t = acc + b_ref[...]
        o_ref[...] = (y_ref[...].astype(F32) * jax.nn.sigmoid(t) * gate_ref[...].astype(F32)).astype(o_ref.dtype)
    else:
        o_ref[...] = acc.astype(o_ref.dtype)


def _mm_kernel(x_ref, w_ref, *rest, mode, scale, n_out, row_ssq, emit_norm):
    acc = jnp.dot(x_ref[...], w_ref[...].astype(BF16), preferred_element_type=F32)
    _mm_epilogue(acc, rest[:-n_out], rest[-n_out:], mode=mode, scale=scale, row_ssq=row_ssq, emit_norm=emit_norm)


def _matmul(x, w, *, col_off, n, mode, out_dtype, extras=(), scale=1.0, tm=512, tn=1024, row_ssq=None,
            next_gain=None):
    m, k = x.shape
    tm = min(tm, m)
    tn = min(tn, n)
    off = col_off // tn
    col_tiles = n // tn
    tile_spec = pl.BlockSpec((tm, tn), lambda j, i: (i, j))
    in_specs = [pl.BlockSpec((tm, k), lambda j, i: (i, 0)),
                pl.BlockSpec((k, tn), lambda j, i: (0, j + off), pipeline_mode=pl.Buffered(2, use_lookahead=True))]
    operands = [x, w]
    extra_bytes = 0
    if row_ssq is not None:
        operands.append(row_ssq)
        in_specs.append(pl.BlockSpec((tm, row_ssq.shape[1]), lambda j, i: (i, 0)))
    extras = tuple(extras) + ((next_gain.reshape(1, n),) if next_gain is not None else ())
    for e in extras:
        operands.append(e)
        if e.shape[0] == 1:
            width = e.shape[1]
            if width == n:
                in_specs.append(pl.BlockSpec((1, tn), lambda j, i: (0, j)))
            else:
                in_specs.append(pl.BlockSpec((1, width), lambda j, i: (0, 0)))
        else:
            in_specs.append(tile_spec)
            extra_bytes += tm * tn * e.dtype.itemsize
    out_shape = [jax.ShapeDtypeStruct((m, n), out_dtype)]
    out_specs = [tile_spec]
    if next_gain is not None:
        out_shape += [jax.ShapeDtypeStruct((m, n), BF16), jax.ShapeDtypeStruct((m, col_tiles * LANES), F32)]
        out_specs += [tile_spec, pl.BlockSpec((tm, LANES), lambda j, i: (i, j))]
        extra_bytes += tm * tn * 2
    out_bytes = tm * tn * jnp.dtype(out_dtype).itemsize
    est = (2 * (tm * k * 2 + k * tn * w.dtype.itemsize + out_bytes + extra_bytes)
           + k * tn * 2 + 3 * tm * tn * 4)
    tile_kernel = functools.partial(_mm_kernel, mode=mode, scale=scale, n_out=len(out_shape),
                                    row_ssq=(k if row_ssq is not None else None), emit_norm=next_gain is not None)

    def stream(*hbm_refs):
        pltpu.emit_pipeline(tile_kernel, grid=(col_tiles, m // tm), in_specs=in_specs, out_specs=out_specs)(*hbm_refs)

    anywhere = pl.BlockSpec(memory_space=pl.ANY)
    outs = pl.pallas_call(
        stream,
        out_shape=out_shape,
        in_specs=[anywhere] * len(operands),
        out_specs=[anywhere] * len(out_shape),
        compiler_params=pltpu.CompilerParams(vmem_limit_bytes=_vmem_limit(est + (4 << 20))),
        name="mm_" + mode,
    )(*operands)
    return outs[0] if next_gain is None else outs


def _attn_kernel(q_ref, k_ref, v_ref, g_ref, o_ref, acc_ref, carry_ref, *, tile, heads):
    s_len = q_ref.shape[0]
    nt = s_len // tile
    row = lax.broadcasted_iota(jnp.int32, (tile, tile), 0)
    col = lax.broadcasted_iota(jnp.int32, (tile, tile), 1)
    causal = col < row
    neg_later = jnp.where(row > col, -1.0, 0.0).astype(BF16)
    for j in reversed(range(nt)):
        first = j == nt - 1
        lo, hi = j * tile, (j + 1) * tile
        for h in range(heads):
            cols = slice(h * HEAD_DIM, (h + 1) * HEAD_DIM)
            k = k_ref[lo:hi, cols]
            v = v_ref[lo:hi, cols]
            z = lax.dot_general(q_ref[lo:, cols], k, (((1,), (1,)), ((), ())), preferred_element_type=F32)
            sp = jnp.maximum(z, 0.0) + jnp.log2(1.0 + jnp.exp2(-jnp.abs(z)))
            sp_diag = jnp.where(causal, sp[:tile], 0.0)
            sp = sp_diag if first else jnp.concatenate([sp_diag, sp[tile:]], axis=0)
            suffix = jnp.dot(sp.astype(BF16), neg_later, preferred_element_type=F32)
            arg = (z - sp) + suffix
            if not first:
                carry = carry_ref[hi:, cols]
                carry = jnp.concatenate([carry] * (tile // LANES), axis=1)
                arg = jnp.concatenate([arg[:tile], arg[tile:] + carry], axis=0)
            w = jnp.exp2(arg)
            w_diag = jnp.where(causal, w[:tile], 0.0)
            w = w_diag if first else jnp.concatenate([w_diag, w[tile:]], axis=0)
            pv = jnp.dot(w.astype(BF16), v, preferred_element_type=F32)
            total = -jnp.broadcast_to(jnp.sum(sp, axis=1, keepdims=True), (sp.shape[0], LANES))
            acc_ref[lo:hi, cols] = pv[:tile]
            carry_ref[lo:hi, cols] = total[:tile]
            if not first:
                acc_ref[hi:, cols] += pv[tile:]
                carry_ref[hi:, cols] += total[tile:]
    o_ref[...] = (acc_ref[...] * g_ref[...].astype(F32)).astype(o_ref.dtype)


def _attention(q, k, v, gate, *, batch, seq):
    m, d = q.shape
    heads = 2 if d % (2 * HEAD_DIM) == 0 else 1
    width = heads * HEAD_DIM
    tile = min(256, seq)
    r3 = lambda a: a.reshape(batch, seq, d)
    spec = pl.BlockSpec((None, seq, width), lambda b, h: (b, 0, h))
    est = 10 * seq * width * 2 + 2 * seq * width * 4 + 8 * heads * seq * tile * 4
    out = pl.pallas_call(
        functools.partial(_attn_kernel, tile=tile, heads=heads),
        out_shape=jax.ShapeDtypeStruct((batch, seq, d), BF16),
        grid=(batch, d // width),
        in_specs=[spec, spec, spec, spec],
        out_specs=spec,
        scratch_shapes=[pltpu.VMEM((seq, width), F32),
                        pltpu.VMEM((seq, width), F32)],
        compiler_params=pltpu.CompilerParams(
            dimension_semantics=("parallel", "parallel"),
            vmem_limit_bytes=_vmem_limit(est + (8 << 20))),
        name="stickbreak_attn",
    )(r3(q), r3(k), r3(v), r3(gate))
    return out.reshape(m, d)


def _dot_nt_3pass(a, b):
    dims = (((1,), (1,)), ((), ()))
    a_hi = a.astype(BF16)
    b_hi = b.astype(BF16)
    a_lo = (a - a_hi.astype(F32)).astype(BF16)
    b_lo = (b - b_hi.astype(F32)).astype(BF16)
    return (lax.dot_general(a_hi, b_hi, dims, preferred_element_type=F32)
            + lax.dot_general(a_hi, b_lo, dims, preferred_element_type=F32)
            + lax.dot_general(a_lo, b_hi, dims, preferred_element_type=F32))


def _ssm_prep_kernel(p_ref, bre_ref, bim_ref, cre_ref, cim_ref, t_ref, w_ref, vt_ref, lam_ref):
    n_rows = 24
    n = lax.broadcasted_iota(jnp.int32, (n_rows, LANES), 0).astype(F32)
    lane = lax.broadcasted_iota(jnp.int32, (GROUP, LANES), 1)
    own_half = (lane < STATE, lane >= STATE)

    def per_pair(pi, _):
        a_re = p_ref[pi, 0:1, :]
        a_im = p_ref[pi, 1:2, :]
        dt = jnp.exp(p_ref[pi, 2:3, :])
        mag = jnp.exp(n * (a_re * dt))
        ang = n * (a_im * dt)
        pre = mag * jnp.cos(ang)
        pim = mag * jnp.sin(ang)
        lam_re = pre[1:2, :]
        lam_im = pim[1:2, :]
        den = a_re * a_re + a_im * a_im
        f_re = ((lam_re - 1.0) * a_re + lam_im * a_im) / den
        f_im = (lam_im * a_re - (lam_re - 1.0) * a_im) / den
        fre = f_re * pre - f_im * pim
        fim = f_re * pim + f_im * pre
        bre = bre_ref[pi]
        bim = bim_ref[pi]
        cre = cre_ref[pi]
        cim = cim_ref[pi]

        v_pair = []
        for e in range(CHUNK + 1):
            v_pair.append((cre * pre[e:e + 1, :] - cim * pim[e:e + 1, :],
                           -(cre * pim[e:e + 1, :] + cim * pre[e:e + 1, :])))
        w_pair = []
        for e in range(CHUNK):
            w_pair.append((fre[e:e + 1, :] * bre - fim[e:e + 1, :] * bim,
                           fre[e:e + 1, :] * bim + fim[e:e + 1, :] * bre))
        lam_pair = jnp.concatenate([pre[CHUNK:CHUNK + 1, :], pim[CHUNK:CHUNK + 1, :],
                                    jnp.zeros((6, LANES), F32)], axis=0)

        for parity in range(2):
            gi = 2 * pi + parity
            keep = own_half[parity]
            own = lambda ab: jnp.concatenate([jnp.where(keep, ab[0], 0.0), jnp.where(keep, ab[1], 0.0)], axis=1)
            v_blocks = [own(v) for v in v_pair]
            for t in range(CHUNK):
                vt_ref[gi, t * GROUP:(t + 1) * GROUP, :] = v_blocks[t + 1].astype(vt_ref.dtype)
            v0t = jnp.concatenate(v_blocks[:CHUNK], axis=0)
            for s in range(CHUNK):
                w_ref[gi, s * GROUP:(s + 1) * GROUP, :] = own(w_pair[CHUNK - 1 - s]).astype(w_ref.dtype)
            wb0 = own(w_pair[0])
            kmat = _dot_nt_3pass(wb0, v0t)
            k0 = kmat[:, :LANES]
            k1 = kmat[:, LANES:]
            zero = jnp.zeros_like(k0)
            for s in range(CHUNK):
                sh = s * GROUP
                if s == 0:
                    lo, hi = k0, k1
                elif sh < LANES:
                    r0 = pltpu.roll(k0, sh, 1)
                    r1 = pltpu.roll(k1, sh, 1)
                    lo = jnp.where(lane >= sh, r0, 0.0)
                    hi = jnp.where(lane >= sh, r1, r0)
                elif sh == LANES:
                    lo, hi = zero, k0
                else:
                    r0 = pltpu.roll(k0, sh - LANES, 1)
                    lo = zero
                    hi = jnp.where(lane >= sh - LANES, r0, 0.0)
                t_ref[gi, s * GROUP:(s + 1) * GROUP, :] = jnp.concatenate([lo, hi], axis=1).astype(t_ref.dtype)
            lam_ref[gi] = jnp.where(keep[:8], lam_pair, 0.0)
        return 0

    lax.fori_loop(0, p_ref.shape[0], per_pair, 0, unroll=True)


def _ssm_prep(a_re, a_im, log_dt, b_re, b_im, c_re, c_im):
    g = a_re.shape[0]

    def pair_lanes(a):
        r = a.shape[1]
        return a.reshape(g // 2, 2, r, STATE).transpose(0, 2, 1, 3).reshape(g // 2, r, LANES)

    rows = jnp.stack([a_re, a_im, jnp.broadcast_to(log_dt[:, None], (g, STATE))]
                     + [jnp.zeros((g, STATE), F32)] * 5, axis=1).astype(F32)
    params = pair_lanes(rows)
    bt_re = pair_lanes(jnp.swapaxes(b_re, 1, 2))
    bt_im = pair_lanes(jnp.swapaxes(b_im, 1, 2))
    cp_re = pair_lanes(c_re)
    cp_im = pair_lanes(c_im)
    gb = GROUPS_PER_TILE
    kk = CHUNK * GROUP
    spec8p = pl.BlockSpec((gb // 2, 8, LANES), lambda i: (i, 0, 0))
    spec16p = pl.BlockSpec((gb // 2, GROUP, LANES), lambda i: (i, 0, 0))
    spec8 = pl.BlockSpec((gb, 8, LANES), lambda i: (i, 0, 0))
    specm = pl.BlockSpec((gb, kk, kk), lambda i: (i, 0, 0))
    return pl.pallas_call(
        _ssm_prep_kernel,
        out_shape=(jax.ShapeDtypeStruct((g, kk, kk), BF16),
                   jax.ShapeDtypeStruct((g, kk, 2 * LANES), BF16),
                   jax.ShapeDtypeStruct((g, kk, 2 * LANES), BF16),
                   jax.ShapeDtypeStruct((g, 8, LANES), F32)),
        grid=(g // gb,),
        in_specs=[spec8p, spec16p, spec16p, spec16p, spec16p],
        out_specs=(specm, specm, specm, spec8),
        compiler_params=pltpu.CompilerParams(dimension_semantics=("parallel",)),
        name="ssm_prep",
    )(params, bt_re, bt_im, cp_re, cp_im)


def _ssm_kernel(u_ref, t_ref, w_ref, vt_ref, lam_ref, d_ref, o_ref, zs, ys, ss, hs, yscr, *, batch, nchunk,
                block_rows):
    n_rows = batch * nchunk
    pairs = GROUPS_PER_TILE // 2
    pitch = nchunk + SCAN_PITCH_PAD
    lane_group = lax.broadcasted_iota(jnp.int32, (block_rows, LANES), 1) // GROUP
    bit_set = {d: (lane_group & d) != 0 for d in (4, 2, 1)}

    def transpose_pieces(xs):
        for d in (4, 2, 1):
            new = list(xs)
            for m in range(GROUPS_PER_TILE):
                if m & d:
                    continue
                a, b = xs[m], xs[m + d]
                new[m] = jnp.where(bit_set[d], pltpu.roll(b, GROUP * d, 1), a)
                new[m + d] = jnp.where(bit_set[d], b, pltpu.roll(a, LANES - GROUP * d, 1))
            xs = new
        return xs

    def position_rows(r0, t):
        return pl.ds(r0 * CHUNK + t, block_rows, stride=CHUNK)

    row_starts = range(0, n_rows, block_rows)

    for r0 in row_starts:
        for half in range(2):
            pieces = transpose_pieces([u_ref[position_rows(r0, half * 8 + m), :] for m in range(GROUPS_PER_TILE)])
            for g in range(GROUPS_PER_TILE):
                zs[g, r0:r0 + block_rows, half * LANES:(half + 1) * LANES] = pieces[g].astype(zs.dtype)

    for p in range(pairs):
        s_end = (jnp.dot(zs[2 * p], w_ref[2 * p], preferred_element_type=F32)
                 + jnp.dot(zs[2 * p + 1], w_ref[2 * p + 1], preferred_element_type=F32))
        for b in range(batch):
            ss[2 * p, b * pitch:b * pitch + nchunk, :] = s_end[b * nchunk:(b + 1) * nchunk, :LANES]
            ss[2 * p + 1, b * pitch:b * pitch + nchunk, :] = s_end[b * nchunk:(b + 1) * nchunk, LANES:]

    lre = [lam_ref[2 * p, 0:1, :] + lam_ref[2 * p + 1, 0:1, :] for p in range(pairs)]
    lim = [lam_ref[2 * p, 1:2, :] + lam_ref[2 * p + 1, 1:2, :] for p in range(pairs)]

    def scan(c, carry):
        new = []
        for p in range(pairs):
            h_re, h_im = carry[2 * p], carry[2 * p + 1]
            rows = pl.ds(c, batch, stride=pitch)
            hs[2 * p, rows, :] = h_re
            hs[2 * p + 1, rows, :] = h_im
            s_re = ss[2 * p, rows, :]
            s_im = ss[2 * p + 1, rows, :]
            new.append(h_re * lre[p] - h_im * lim[p] + s_re)
            new.append(h_im * lre[p] + h_re * lim[p] + s_im)
        return tuple(new)

    zero_state = jnp.zeros((batch, LANES), F32)
    lax.fori_loop(0, nchunk, scan, tuple(zero_state for _ in range(2 * pairs)))

    def entering(q):
        return jnp.concatenate([hs[q, b * pitch:b * pitch + nchunk, :] for b in range(batch)], axis=0)

    for g in range(GROUPS_PER_TILE):
        h_in = jnp.concatenate([entering(2 * (g // 2)), entering(2 * (g // 2) + 1)], axis=1).astype(BF16)
        ys[g] = (jnp.dot(zs[g], t_ref[g], preferred_element_type=F32)
                 + lax.dot_general(h_in, vt_ref[g], (((1,), (1,)), ((), ())), preferred_element_type=F32))

    d_row = d_ref[...]
    for r0 in row_starts:
        for half in range(2):
            pieces = transpose_pieces([ys[g, r0:r0 + block_rows, half * LANES:(half + 1) * LANES]
                                       for g in range(GROUPS_PER_TILE)])
            for m in range(GROUPS_PER_TILE):
                rows = position_rows(r0, half * 8 + m)
                yscr[rows, :] = jax.nn.gelu(pieces[m] + d_row * u_ref[rows, :])

    o_ref[...] = yscr[...].astype(o_ref.dtype)


def _ssm(u, tmat, wmat, vtmat, lam, d_vec, *, batch, seq):
    m, d = u.shape
    nchunk = seq // CHUNK
    n_rows = batch * nchunk
    n_pad = batch * (nchunk + SCAN_PITCH_PAD)
    gb = GROUPS_PER_TILE
    kk = CHUNK * GROUP
    block_rows = min(n_rows, 128)
    specm = pl.BlockSpec((gb, kk, kk), lambda i: (i, 0, 0))
    est = (2 * m * LANES * (4 + 2) + 2 * 3 * gb * kk * kk * 2 + gb * n_rows * kk * (2 + 4)
           + 2 * gb * n_pad * LANES * 4 + m * LANES * 4 + 8 * n_rows * kk * 4)
    return pl.pallas_call(
        functools.partial(_ssm_kernel, batch=batch, nchunk=nchunk, block_rows=block_rows),
        out_shape=jax.ShapeDtypeStruct((m, d), BF16),
        grid=(d // LANES,),
        in_specs=[pl.BlockSpec((m, LANES), lambda i: (0, i)),
                  specm, specm, specm,
                  pl.BlockSpec((gb, 8, LANES), lambda i: (i, 0, 0)),
                  pl.BlockSpec((1, LANES), lambda i: (0, i))],
        out_specs=pl.BlockSpec((m, LANES), lambda i: (0, i)),
        scratch_shapes=[pltpu.VMEM((gb, n_rows, kk), BF16),
                        pltpu.VMEM((gb, n_rows, kk), F32),
                        pltpu.VMEM((gb, n_pad, LANES), F32),
                        pltpu.VMEM((gb, n_pad, LANES), F32),
                        pltpu.VMEM((m, LANES), F32)],
        compiler_params=pltpu.CompilerParams(
            dimension_semantics=("parallel",),
            vmem_limit_bytes=_vmem_limit(est)),
        name="s5_scan",
    )(u, tmat, wmat, vtmat, lam, d_vec.reshape(1, d))


def kernel(x, norm_g, attn_w_in, attn_q_g, attn_k_g, attn_w_out, ssm_w_in, ssm_A_re, ssm_A_im, ssm_log_dt,
           ssm_B_re, ssm_B_im, ssm_C_re, ssm_C_im, ssm_D, ssm_glu_w, ssm_glu_b, ssm_w_out):
    batch, seq, d = x.shape
    m = batch * seq
    assert d % LANES == 0 and seq % 256 == 0 and (m // CHUNK) % 8 == 0
    assert ssm_A_re.shape[1:] == (d // GROUP, STATE) and (d // GROUP) % GROUPS_PER_TILE == 0
    x2 = x.reshape(m, d)

    h = _rmsnorm(x2, norm_g[0])
    w_in = attn_w_in[0]
    qg = attn_q_g[0].reshape(1, HEAD_DIM)
    kg = attn_k_g[0].reshape(1, HEAD_DIM)
    q = _matmul(h, w_in, col_off=0, n=d, mode="headnorm", out_dtype=BF16, extras=(qg,),
                scale=LOG2E / math.sqrt(HEAD_DIM))
    k = _matmul(h, w_in, col_off=d, n=d, mode="headnorm", out_dtype=BF16, extras=(kg,))
    v = _matmul(h, w_in, col_off=2 * d, n=d, mode="store", out_dtype=BF16)
    gate = _matmul(h, w_in, col_off=3 * d, n=d, mode="silu", out_dtype=BF16)
    og = _attention(q, k, v, gate, batch=batch, seq=seq)
    x2, h, ssq = _matmul(og, attn_w_out[0], col_off=0, n=d, mode="residual", out_dtype=F32, extras=(x2,),
                         next_gain=norm_g[1])

    w_in = ssm_w_in[0]
    u = _matmul(h, w_in, col_off=0, n=d, mode="store", out_dtype=F32, row_ssq=ssq)
    gate = _matmul(h, w_in, col_off=d, n=d, mode="silu", out_dtype=BF16, row_ssq=ssq)
    tmat, wmat, vtmat, lam = _ssm_prep(ssm_A_re[0], ssm_A_im[0], ssm_log_dt[0], ssm_B_re[0], ssm_B_im[0],
                                       ssm_C_re[0], ssm_C_im[0])
    y = _ssm(u, tmat, wmat, vtmat, lam, ssm_D[0], batch=batch, seq=seq)
    yg = _matmul(y, ssm_glu_w[0], col_off=0, n=d, mode="glu", out_dtype=BF16,
                 extras=(ssm_glu_b[0].reshape(1, d), y, gate))
    x2 = _matmul(yg, ssm_w_out[0], col_off=0, n=d, mode="residual", out_dtype=F32, extras=(x2,))
    return x2.reshape(batch, seq, d)
```

```python
import functools
import math

import jax
import jax.numpy as jnp
from jax import lax
from jax.experimental import pallas as pl
from jax.experimental.pallas import tpu as pltpu

HEAD_DIM = 128
GROUP = 16
STATE = 64
RMS_EPS = 1e-6
LANES = 128
CHUNK = 16
LOG2E = 1.4426950408889634
SCAN_PITCH_PAD = 8
GROUPS_PER_TILE = LANES // GROUP
VMEM_CAP_BYTES = 60 * 1024 * 1024

F32 = jnp.float32
BF16 = jnp.bfloat16


def _vmem_limit(estimate_bytes):
    return int(min(VMEM_CAP_BYTES, max(32 * 1024 * 1024, estimate_bytes)))


def _rmsnorm_kernel(x_ref, g_ref, o_ref):
    x = x_ref[...]
    r = lax.rsqrt(jnp.mean(x * x, axis=-1, keepdims=True) + RMS_EPS)
    o_ref[...] = ((x * r) * g_ref[...]).astype(o_ref.dtype)


def _rmsnorm(x2, g):
    m, d = x2.shape
    tm = min(512, m)
    return pl.pallas_call(
        _rmsnorm_kernel,
        out_shape=jax.ShapeDtypeStruct((m, d), BF16),
        grid=(m // tm,),
        in_specs=[pl.BlockSpec((tm, d), lambda i: (i, 0)),
                  pl.BlockSpec((1, d), lambda i: (0, 0))],
        out_specs=pl.BlockSpec((tm, d), lambda i: (i, 0)),
        compiler_params=pltpu.CompilerParams(dimension_semantics=("parallel",),
                                             vmem_limit_bytes=_vmem_limit(2 * tm * d * (4 + 2) + 4 * tm * d * 4)),
        name="rmsnorm",
    )(x2, g.reshape(1, d))


def _mm_epilogue(acc, extras, outs, *, mode, scale, row_ssq, emit_norm):
    extras = list(extras)
    if row_ssq:
        ssq = extras.pop(0)[...]
        total = sum(ssq[:, c * LANES:(c + 1) * LANES] for c in range(ssq.shape[1] // LANES))
        r = lax.rsqrt(total * (1.0 / row_ssq) + RMS_EPS)
        acc = acc * jnp.concatenate([r] * (acc.shape[1] // LANES), axis=1)
    o_ref = outs[0]
    if mode == "headnorm":
        g = extras[0][...]
        for h in range(acc.shape[1] // HEAD_DIM):
            sl = slice(h * HEAD_DIM, (h + 1) * HEAD_DIM)
            a = acc[:, sl]
            r = lax.rsqrt(jnp.mean(a * a, axis=-1, keepdims=True) + RMS_EPS)
            o_ref[:, sl] = (((a * r) * g) * scale).astype(o_ref.dtype)
    elif mode == "silu":
        half = 0.5 * acc
        o_ref[...] = (half * jnp.tanh(half) + half).astype(o_ref.dtype)
    elif mode == "residual":
        y = extras[0][...] + acc
        o_ref[...] = y.astype(o_ref.dtype)
        if emit_norm:
            outs[1][...] = (y * extras[1][...]).astype(outs[1].dtype)
            outs[2][...] = jnp.broadcast_to(jnp.sum(y * y, axis=1, keepdims=True), outs[2].shape)
    elif mode == "glu":
        b_ref, y_ref, gate_ref = extras[0], extras[1], extras[2]
        t = acc + b_ref[...]
        o_ref[...] = (y_ref[...].astype(F32) * jax.nn.sigmoid(t) * gate_ref[...].astype(F32)).astype(o_ref.dtype)
    else:
        o_ref[...] = acc.astype(o_ref.dtype)


def _mm_kernel(x_ref, w_ref, *rest, mode, scale, n_out, row_ssq, emit_norm):
    acc = jnp.dot(x_ref[...], w_ref[...].astype(BF16), preferred_element_type=F32)
    _mm_epilogue(acc, rest[:-n_out], rest[-n_out:], mode=mode, scale=scale, row_ssq=row_ssq, emit_norm=emit_norm)


def _matmul(x, w, *, col_off, n, mode, out_dtype, extras=(), scale=1.0, tm=512, tn=1024, row_ssq=None,
            next_gain=None):
    m, k = x.shape
    tm = min(tm, m)
    tn = min(tn, n)
    off = col_off // tn
    col_tiles = n // tn
    tile_spec = pl.BlockSpec((tm, tn), lambda j, i: (i, j))
    in_specs = [pl.BlockSpec((tm, k), lambda j, i: (i, 0)),
                pl.BlockSpec((k, tn), lambda j, i: (0, j + off), pipeline_mode=pl.Buffered(2, use_lookahead=True))]
    operands = [x, w]
    extra_bytes = 0
    if row_ssq is not None:
        operands.append(row_ssq)
        in_specs.append(pl.BlockSpec((tm, row_ssq.shape[1]), lambda j, i: (i, 0)))
    extras = tuple(extras) + ((next_gain.reshape(1, n),) if next_gain is not None else ())
    for e in extras:
        operands.append(e)
        if e.shape[0] == 1:
            width = e.shape[1]
            if width == n:
                in_specs.append(pl.BlockSpec((1, tn), lambda j, i: (0, j)))
            else:
                in_specs.append(pl.BlockSpec((1, width), lambda j, i: (0, 0)))
        else:
            in_specs.append(tile_spec)
            extra_bytes += tm * tn * e.dtype.itemsize
    out_shape = [jax.ShapeDtypeStruct((m, n), out_dtype)]
    out_specs = [tile_spec]
    if next_gain is not None:
        out_shape += [jax.ShapeDtypeStruct((m, n), BF16), jax.ShapeDtypeStruct((m, col_tiles * LANES), F32)]
        out_specs += [tile_spec, pl.BlockSpec((tm, LANES), lambda j, i: (i, j))]
        extra_bytes += tm * tn * 2
    out_bytes = tm * tn * jnp.dtype(out_dtype).itemsize
    est = (2 * (tm * k * 2 + k * tn * w.dtype.itemsize + out_bytes + extra_bytes)
           + k * tn * 2 + 3 * tm * tn * 4)
    tile_kernel = functools.partial(_mm_kernel, mode=mode, scale=scale, n_out=len(out_shape),
                                    row_ssq=(k if row_ssq is not None else None), emit_norm=next_gain is not None)

    def stream(*hbm_refs):
        pltpu.emit_pipeline(tile_kernel, grid=(col_tiles, m // tm), in_specs=in_specs, out_specs=out_specs)(*hbm_refs)

    anywhere = pl.BlockSpec(memory_space=pl.ANY)
    outs = pl.pallas_call(
        stream,
        out_shape=out_shape,
        in_specs=[anywhere] * len(operands),
        out_specs=[anywhere] * len(out_shape),
        compiler_params=pltpu.CompilerParams(vmem_limit_bytes=_vmem_limit(est + (4 << 20))),
        name="mm_" + mode,
    )(*operands)
    return outs[0] if next_gain is None else outs


def _attn_kernel(q_ref, k_ref, v_ref, g_ref, o_ref, acc_ref, carry_ref, *, tile, heads):
    s_len = q_ref.shape[0]
    nt = s_len // tile
    row = lax.broadcasted_iota(jnp.int32, (tile, tile), 0)
    col = lax.broadcasted_iota(jnp.int32, (tile, tile), 1)
    causal = col < row
    neg_later = jnp.where(row > col, -1.0, 0.0).astype(BF16)
    for j in reversed(range(nt)):
        first = j == nt - 1
        lo, hi = j * tile, (j + 1) * tile
        for h in range(heads):
            cols = slice(h * HEAD_DIM, (h + 1) * HEAD_DIM)
            k = k_ref[lo:hi, cols]
            v = v_ref[lo:hi, cols]
            z = lax.dot_general(q_ref[lo:, cols], k, (((1,), (1,)), ((), ())), preferred_element_type=F32)
            sp = jnp.maximum(z, 0.0) + jnp.log2(1.0 + jnp.exp2(-jnp.abs(z)))
            sp_diag = jnp.where(causal, sp[:tile], 0.0)
            sp = sp_diag if first else jnp.concatenate([sp_diag, sp[tile:]], axis=0)
            suffix = jnp.dot(sp.astype(BF16), neg_later, preferred_element_type=F32)
            arg = (z - sp) + suffix
            if not first:
                carry = carry_ref[hi:, cols]
                carry = jnp.concatenate([carry] * (tile // LANES), axis=1)
                arg = jnp.concatenate([arg[:tile], arg[tile:] + carry], axis=0)
            w = jnp.exp2(arg)
            w_diag = jnp.where(causal, w[:tile], 0.0)
            w = w_diag if first else jnp.concatenate([w_diag, w[tile:]], axis=0)
            pv = jnp.dot(w.astype(BF16), v, preferred_element_type=F32)
            total = -jnp.broadcast_to(jnp.sum(sp, axis=1, keepdims=True), (sp.shape[0], LANES))
            acc_ref[lo:hi, cols] = pv[:tile]
            carry_ref[lo:hi, cols] = total[:tile]
            if not first:
                acc_ref[hi:, cols] += pv[tile:]
                carry_ref[hi:, cols] += total[tile:]
    o_ref[...] = (acc_ref[...] * g_ref[...].astype(F32)).astype(o_ref.dtype)


def _attention(q, k, v, gate, *, batch, seq):
    m, d = q.shape
    heads = 2 if d % (2 * HEAD_DIM) == 0 else 1
    width = heads * HEAD_DIM
    tile = min(256, seq)
    r3 = lambda a: a.reshape(batch, seq, d)
    spec = pl.BlockSpec((None, seq, width), lambda b, h: (b, 0, h))
    est = 10 * seq * width * 2 + 2 * seq * width * 4 + 8 * heads * seq * tile * 4
    out = pl.pallas_call(
        functools.partial(_attn_kernel, tile=tile, heads=heads),
        out_shape=jax.ShapeDtypeStruct((batch, seq, d), BF16),
        grid=(batch, d // width),
        in_specs=[spec, spec, spec, spec],
        out_specs=spec,
        scratch_shapes=[pltpu.VMEM((seq, width), F32),
                        pltpu.VMEM((seq, width), F32)],
        compiler_params=pltpu.CompilerParams(
            dimension_semantics=("parallel", "parallel"),
            vmem_limit_bytes=_vmem_limit(est + (8 << 20))),
        name="stickbreak_attn",
    )(r3(q), r3(k), r3(v), r3(gate))
    return out.reshape(m, d)


def _dot_nt_3pass(a, b):
    dims = (((1,), (1,)), ((), ()))
    a_hi = a.astype(BF16)
    b_hi = b.astype(BF16)
    a_lo = (a - a_hi.astype(F32)).astype(BF16)
    b_lo = (b - b_hi.astype(F32)).astype(BF16)
    return (lax.dot_general(a_hi, b_hi, dims, preferred_element_type=F32)
            + lax.dot_general(a_hi, b_lo, dims, preferred_element_type=F32)
            + lax.dot_general(a_lo, b_hi, dims, preferred_element_type=F32))


def _ssm_prep_kernel(p_ref, bre_ref, bim_ref, cre_ref, cim_ref, t_ref, w_ref, vt_ref, lam_ref):
    n_rows = 24
    n = lax.broadcasted_iota(jnp.int32, (n_rows, LANES), 0).astype(F32)
    lane = lax.broadcasted_iota(jnp.int32, (GROUP, LANES), 1)
    own_half = (lane < STATE, lane >= STATE)

    def per_pair(pi, _):
        a_re = p_ref[pi, 0:1, :]
        a_im = p_ref[pi, 1:2, :]
        dt = jnp.exp(p_ref[pi, 2:3, :])
        mag = jnp.exp(n * (a_re * dt))
        ang = n * (a_im * dt)
        pre = mag * jnp.cos(ang)
        pim = mag * jnp.sin(ang)
        lam_re = pre[1:2, :]
        lam_im = pim[1:2, :]
        den = a_re * a_re + a_im * a_im
        f_re = ((lam_re - 1.0) * a_re + lam_im * a_im) / den
        f_im = (lam_im * a_re - (lam_re - 1.0) * a_im) / den
        fre = f_re * pre - f_im * pim
        fim = f_re * pim + f_im * pre
        bre = bre_ref[pi]
        bim = bim_ref[pi]
        cre = cre_ref[pi]
        cim = cim_ref[pi]

        v_pair = []
        for e in range(CHUNK + 1):
            v_pair.append((cre * pre[e:e + 1, :] - cim * pim[e:e + 1, :],
                           -(cre * pim[e:e + 1, :] + cim * pre[e:e + 1, :])))
        w_pair = []
        for e in range(CHUNK):
            w_pair.append((fre[e:e + 1, :] * bre - fim[e:e + 1, :] * bim,
                           fre[e:e + 1, :] * bim + fim[e:e + 1, :] * bre))
        lam_pair = jnp.concatenate([pre[CHUNK:CHUNK + 1, :], pim[CHUNK:CHUNK + 1, :],
                                    jnp.zeros((6, LANES), F32)], axis=0)

        for parity in range(2):
            gi = 2 * pi + parity
            keep = own_half[parity]
            own = lambda ab: jnp.concatenate([jnp.where(keep, ab[0], 0.0), jnp.where(keep, ab[1], 0.0)], axis=1)
            v_blocks = [own(v) for v in v_pair]
            for t in range(CHUNK):
                vt_ref[gi, t * GROUP:(t + 1) * GROUP, :] = v_blocks[t + 1].astype(vt_ref.dtype)
            v0t = jnp.concatenate(v_blocks[:CHUNK], axis=0)
            for s in range(CHUNK):
                w_ref[gi, s * GROUP:(s + 1) * GROUP, :] = own(w_pair[CHUNK - 1 - s]).astype(w_ref.dtype)
            wb0 = own(w_pair[0])
            kmat = _dot_nt_3pass(wb0, v0t)
            k0 = kmat[:, :LANES]
            k1 = kmat[:, LANES:]
            zero = jnp.zeros_like(k0)
            for s in range(CHUNK):
                sh = s * GROUP
                if s == 0:
                    lo, hi = k0, k1
                elif sh < LANES:
                    r0 = pltpu.roll(k0, sh, 1)
                    r1 = pltpu.roll(k1, sh, 1)
                    lo = jnp.where(lane >= sh, r0, 0.0)
                    hi = jnp.where(lane >= sh, r1, r0)
                elif sh == LANES:
                    lo, hi = zero, k0
                else:
                    r0 = pltpu.roll(k0, sh - LANES, 1)
                    lo = zero
                    hi = jnp.where(lane >= sh - LANES, r0, 0.0)
                t_ref[gi, s * GROUP:(s + 1) * GROUP, :] = jnp.concatenate([lo, hi], axis=1).astype(t_ref.dtype)
            lam_ref[gi] = jnp.where(keep[:8], lam_pair, 0.0)
        return 0

    lax.fori_loop(0, p_ref.shape[0], per_pair, 0, unroll=True)


def _ssm_prep(a_re, a_im, log_dt, b_re, b_im, c_re, c_im):
    g = a_re.shape[0]

    def pair_lanes(a):
        r = a.shape[1]
        return a.reshape(g // 2, 2, r, STATE).transpose(0, 2, 1, 3).reshape(g // 2, r, LANES)

    rows = jnp.stack([a_re, a_im, jnp.broadcast_to(log_dt[:, None], (g, STATE))]
                     + [jnp.zeros((g, STATE), F32)] * 5, axis=1).astype(F32)
    params = pair_lanes(rows)
    bt_re = pair_lanes(jnp.swapaxes(b_re, 1, 2))
    bt_im = pair_lanes(jnp.swapaxes(b_im, 1, 2))
    cp_re = pair_lanes(c_re)
    cp_im = pair_lanes(c_im)
    gb = GROUPS_PER_TILE
    kk = CHUNK * GROUP
    spec8p = pl.BlockSpec((gb // 2, 8, LANES), lambda i: (i, 0, 0))
    spec16p = pl.BlockSpec((gb // 2, GROUP, LANES), lambda i: (i, 0, 0))
    spec8 = pl.BlockSpec((gb, 8, LANES), lambda i: (i, 0, 0))
    specm = pl.BlockSpec((gb, kk, kk), lambda i: (i, 0, 0))
    return pl.pallas_call(
        _ssm_prep_kernel,
        out_shape=(jax.ShapeDtypeStruct((g, kk, kk), BF16),
                   jax.ShapeDtypeStruct((g, kk, 2 * LANES), BF16),
                   jax.ShapeDtypeStruct((g, kk, 2 * LANES), BF16),
                   jax.ShapeDtypeStruct((g, 8, LANES), F32)),
        grid=(g // gb,),
        in_specs=[spec8p, spec16p, spec16p, spec16p, spec16p],
        out_specs=(specm, specm, specm, spec8),
        compiler_params=pltpu.CompilerParams(dimension_semantics=("parallel",)),
        name="ssm_prep",
    )(params, bt_re, bt_im, cp_re, cp_im)


def _ssm_kernel(u_ref, t_ref, w_ref, vt_ref, lam_ref, d_ref, o_ref, zs, ys, ss, hs, yscr, *, batch, nchunk,
                block_rows):
    n_rows = batch * nchunk
    pairs = GROUPS_PER_TILE // 2
    pitch = nchunk + SCAN_PITCH_PAD
    lane_group = lax.broadcasted_iota(jnp.int32, (block_rows, LANES), 1) // GROUP
    bit_set = {d: (lane_group & d) != 0 for d in (4, 2, 1)}

    def transpose_pieces(xs, rotate=pltpu.roll):
        for d in (4, 2, 1):
            new = list(xs)
            for m in range(GROUPS_PER_TILE):
                if m & d:
                    continue
                a, b = xs[m], xs[m + d]
                new[m] = jnp.where(bit_set[d], rotate(b, GROUP * d, 1), a)
                new[m + d] = jnp.where(bit_set[d], b, rotate(a, LANES - GROUP * d, 1))
            xs = new
        return xs

    def position_rows(r0, t):
        return pl.ds(r0 * CHUNK + t, block_rows, stride=CHUNK)

    row_starts = range(0, n_rows, block_rows)

    for r0 in row_starts:
        for half in range(2):
            pieces = transpose_pieces([u_ref[position_rows(r0, half * 8 + m), :].astype(zs.dtype)
                                       for m in range(GROUPS_PER_TILE)], jnp.roll)
            for g in range(GROUPS_PER_TILE):
                zs[g, r0:r0 + block_rows, half * LANES:(half + 1) * LANES] = pieces[g]

    for p in range(pairs):
        s_end = (jnp.dot(zs[2 * p], w_ref[2 * p], preferred_element_type=F32)
                 + jnp.dot(zs[2 * p + 1], w_ref[2 * p + 1], preferred_element_type=F32))
        for b in range(batch):
            ss[2 * p, b * pitch:b * pitch + nchunk, :] = s_end[b * nchunk:(b + 1) * nchunk, :LANES]
            ss[2 * p + 1, b * pitch:b * pitch + nchunk, :] = s_end[b * nchunk:(b + 1) * nchunk, LANES:]

    lre = [lam_ref[2 * p, 0:1, :] + lam_ref[2 * p + 1, 0:1, :] for p in range(pairs)]
    lim = [lam_ref[2 * p, 1:2, :] + lam_ref[2 * p + 1, 1:2, :] for p in range(pairs)]

    def scan(c, carry):
        new = []
        for p in range(pairs):
            h_re, h_im = carry[2 * p], carry[2 * p + 1]
            rows = pl.ds(c, batch, stride=pitch)
            hs[2 * p, rows, :] = h_re
            hs[2 * p + 1, rows, :] = h_im
            s_re = ss[2 * p, rows, :]
            s_im = ss[2 * p + 1, rows, :]
            new.append(h_re * lre[p] - h_im * lim[p] + s_re)
            new.append(h_im * lre[p] + h_re * lim[p] + s_im)
        return tuple(new)

    zero_state = jnp.zeros((batch, LANES), F32)
    lax.fori_loop(0, nchunk, scan, tuple(zero_state for _ in range(2 * pairs)))

    def entering(q):
        return jnp.concatenate([hs[q, b * pitch:b * pitch + nchunk, :] for b in range(batch)], axis=0)

    for g in range(GROUPS_PER_TILE):
        h_in = jnp.concatenate([entering(2 * (g // 2)), entering(2 * (g // 2) + 1)], axis=1).astype(BF16)
        ys[g] = (jnp.dot(zs[g], t_ref[g], preferred_element_type=F32)
                 + lax.dot_general(h_in, vt_ref[g], (((1,), (1,)), ((), ())), preferred_element_type=F32))

    d_row = d_ref[...]
    for r0 in row_starts:
        for half in range(2):
            pieces = transpose_pieces([ys[g, r0:r0 + block_rows, half * LANES:(half + 1) * LANES]
                                       for g in range(GROUPS_PER_TILE)])
            for m in range(GROUPS_PER_TILE):
                rows = position_rows(r0, half * 8 + m)
                yscr[rows, :] = jax.nn.gelu(pieces[m] + d_row * u_ref[rows, :])

    o_ref[...] = yscr[...].astype(o_ref.dtype)


def _ssm(u, tmat, wmat, vtmat, lam, d_vec, *, batch, seq):
    m, d = u.shape
    nchunk = seq // CHUNK
    n_rows = batch * nchunk
    n_pad = batch * (nchunk + SCAN_PITCH_PAD)
    gb = GROUPS_PER_TILE
    kk = CHUNK * GROUP
    block_rows = min(n_rows, 128)
    specm = pl.BlockSpec((gb, kk, kk), lambda i: (i, 0, 0))
    est = (2 * m * LANES * (4 + 2) + 2 * 3 * gb * kk * kk * 2 + gb * n_rows * kk * (2 + 4)
           + 2 * gb * n_pad * LANES * 4 + m * LANES * 4 + 8 * n_rows * kk * 4)
    return pl.pallas_call(
        functools.partial(_ssm_kernel, batch=batch, nchunk=nchunk, block_rows=block_rows),
        out_shape=jax.ShapeDtypeStruct((m, d), BF16),
        grid=(d // LANES,),
        in_specs=[pl.BlockSpec((m, LANES), lambda i: (0, i)),
                  specm, specm, specm,
                  pl.BlockSpec((gb, 8, LANES), lambda i: (i, 0, 0)),
                  pl.BlockSpec((1, LANES), lambda i: (0, i))],
        out_specs=pl.BlockSpec((m, LANES), lambda i: (0, i)),
        scratch_shapes=[pltpu.VMEM((gb, n_rows, kk), BF16),
                        pltpu.VMEM((gb, n_rows, kk), F32),
                        pltpu.VMEM((gb, n_pad, LANES), F32),
                        pltpu.VMEM((gb, n_pad, LANES), F32),
                        pltpu.VMEM((m, LANES), F32)],
        compiler_params=pltpu.CompilerParams(
            dimension_semantics=("parallel",),
            vmem_limit_bytes=_vmem_limit(est)),
        name="s5_scan",
    )(u, tmat, wmat, vtmat, lam, d_vec.reshape(1, d))


def kernel(x, norm_g, attn_w_in, attn_q_g, attn_k_g, attn_w_out, ssm_w_in, ssm_A_re, ssm_A_im, ssm_log_dt,
           ssm_B_re, ssm_B_im, ssm_C_re, ssm_C_im, ssm_D, ssm_glu_w, ssm_glu_b, ssm_w_out):
    batch, seq, d = x.shape
    m = batch * seq
    assert d % LANES == 0 and seq % 256 == 0 and (m // CHUNK) % 8 == 0
    assert ssm_A_re.shape[1:] == (d // GROUP, STATE) and (d // GROUP) % GROUPS_PER_TILE == 0
    x2 = x.reshape(m, d)

    h = _rmsnorm(x2, norm_g[0])
    w_in = attn_w_in[0]
    qg = attn_q_g[0].reshape(1, HEAD_DIM)
    kg = attn_k_g[0].reshape(1, HEAD_DIM)
    q = _matmul(h, w_in, col_off=0, n=d, mode="headnorm", out_dtype=BF16, extras=(qg,),
                scale=LOG2E / math.sqrt(HEAD_DIM))
    k = _matmul(h, w_in, col_off=d, n=d, mode="headnorm", out_dtype=BF16, extras=(kg,))
    v = _matmul(h, w_in, col_off=2 * d, n=d, mode="store", out_dtype=BF16)
    gate = _matmul(h, w_in, col_off=3 * d, n=d, mode="silu", out_dtype=BF16)
    og = _attention(q, k, v, gate, batch=batch, seq=seq)
    x2, h, ssq = _matmul(og, attn_w_out[0], col_off=0, n=d, mode="residual", out_dtype=F32, extras=(x2,),
                         next_gain=norm_g[1])

    w_in = ssm_w_in[0]
    u = _matmul(h, w_in, col_off=0, n=d, mode="store", out_dtype=F32, row_ssq=ssq)
    gate = _matmul(h, w_in, col_off=d, n=d, mode="silu", out_dtype=BF16, row_ssq=ssq)
    tmat, wmat, vtmat, lam = _ssm_prep(ssm_A_re[0], ssm_A_im[0], ssm_log_dt[0], ssm_B_re[0], ssm_B_im[0],
                                       ssm_C_re[0], ssm_C_im[0])
    y = _ssm(u, tmat, wmat, vtmat, lam, ssm_D[0], batch=batch, seq=seq)
    yg = _matmul(y, ssm_glu_w[0], col_off=0, n=d, mode="glu", out_dtype=BF16,
                 extras=(ssm_glu_b[0].reshape(1, d), y, gate))
    x2 = _matmul(yg, ssm_w_out[0], col_off=0, n=d, mode="residual", out_dtype=F32, extras=(x2,))
    return x2.reshape(batch, seq, d)
```

```python
import functools
import math

import jax
import jax.numpy as jnp
from jax import lax
from jax.experimental import pallas as pl
from jax.experimental.pallas import tpu as pltpu

HEAD_DIM = 128
GROUP = 16
STATE = 64
RMS_EPS = 1e-6
LANES = 128
CHUNK = 16
LOG2E = 1.4426950408889634
SCAN_PITCH_PAD = 8
GROUPS_PER_TILE = LANES // GROUP
VMEM_CAP_BYTES = 60 * 1024 * 1024

F32 = jnp.float32
BF16 = jnp.bfloat16


def _vmem_limit(estimate_bytes):
    return int(min(VMEM_CAP_BYTES, max(32 * 1024 * 1024, estimate_bytes)))


def _rmsnorm_kernel(x_ref, g_ref, o_ref):
    x = x_ref[...]
    r = lax.rsqrt(jnp.mean(x * x, axis=-1, keepdims=True) + RMS_EPS)
    o_ref[...] = ((x * r) * g_ref[...]).astype(o_ref.dtype)


def _rmsnorm(x2, g):
    m, d = x2.shape
    tm = min(512, m)
    return pl.pallas_call(
        _rmsnorm_kernel,
        out_shape=jax.ShapeDtypeStruct((m, d), BF16),
        grid=(m // tm,),
        in_specs=[pl.BlockSpec((tm, d), lambda i: (i, 0)),
                  pl.BlockSpec((1, d), lambda i: (0, 0))],
        out_specs=pl.BlockSpec((tm, d), lambda i: (i, 0)),
        compiler_params=pltpu.CompilerParams(dimension_semantics=("parallel",),
                                             vmem_limit_bytes=_vmem_limit(2 * tm * d * (4 + 2) + 4 * tm * d * 4)),
        name="rmsnorm",
    )(x2, g.reshape(1, d))


def _mm_epilogue(acc, extras, outs, *, mode, scale, row_ssq, emit_norm):
    extras = list(extras)
    if row_ssq:
        ssq = extras.pop(0)[...]
        total = sum(ssq[:, c * LANES:(c + 1) * LANES] for c in range(ssq.shape[1] // LANES))
        r = lax.rsqrt(total * (1.0 / row_ssq) + RMS_EPS)
        acc = acc * jnp.concatenate([r] * (acc.shape[1] // LANES), axis=1)
    o_ref = outs[0]
    if mode == "headnorm":
        g = extras[0][...] * scale
        for h in range(acc.shape[1] // HEAD_DIM):
            sl = slice(h * HEAD_DIM, (h + 1) * HEAD_DIM)
            a = acc[:, sl]
            r = lax.rsqrt(jnp.mean(a * a, axis=-1, keepdims=True) + RMS_EPS)
            o_ref[:, sl] = ((a * r) * g).astype(o_ref.dtype)
    elif mode == "silu":
        half = 0.5 * acc
        o_ref[...] = (half * jnp.tanh(half) + half).astype(o_ref.dtype)
    elif mode == "residual":
        y = extras[0][...] + acc
        o_ref[...] = y.astype(o_ref.dtype)
        if emit_norm:
            outs[1][...] = (y * extras[1][...]).astype(outs[1].dtype)
            outs[2][...] = jnp.broadcast_to(jnp.sum(y * y, axis=1, keepdims=True), outs[2].shape)
    elif mode == "glu":
        b_ref, y_ref, gate_ref = extras[0], extras[1], extras[2]
        t = acc + b_ref[...]
        sig = 0.5 * jnp.tanh(0.5 * t) + 0.5
        o_ref[...] = (y_ref[...].astype(F32) * sig * gate_ref[...].astype(F32)).astype(o_ref.dtype)
    else:
        o_ref[...] = acc.astype(o_ref.dtype)


def _mm_kernel(x_ref, w_ref, *rest, mode, scale, n_out, row_ssq, emit_norm):
    acc = jnp.dot(x_ref[...], w_ref[...].astype(BF16), preferred_element_type=F32)
    _mm_epilogue(acc, rest[:-n_out], rest[-n_out:], mode=mode, scale=scale, row_ssq=row_ssq, emit_norm=emit_norm)


def _matmul(x, w, *, col_off, n, mode, out_dtype, extras=(), scale=1.0, tm=512, tn=1024, row_ssq=None,
            next_gain=None):
    m, k = x.shape
    tm = min(tm, m)
    tn = min(tn, n)
    off = col_off // tn
    col_tiles = n // tn
    tile_spec = pl.BlockSpec((tm, tn), lambda j, i: (i, j))
    in_specs = [pl.BlockSpec((tm, k), lambda j, i: (i, 0)),
                pl.BlockSpec((k, tn), lambda j, i: (0, j + off), pipeline_mode=pl.Buffered(2, use_lookahead=True))]
    operands = [x, w]
    extra_bytes = 0
    if row_ssq is not None:
        operands.append(row_ssq)
        in_specs.append(pl.BlockSpec((tm, row_ssq.shape[1]), lambda j, i: (i, 0)))
    extras = tuple(extras) + ((next_gain.reshape(1, n),) if next_gain is not None else ())
    for e in extras:
        operands.append(e)
        if e.shape[0] == 1:
            width = e.shape[1]
            if width == n:
                in_specs.append(pl.BlockSpec((1, tn), lambda j, i: (0, j)))
            else:
                in_specs.append(pl.BlockSpec((1, width), lambda j, i: (0, 0)))
        else:
            in_specs.append(tile_spec)
            extra_bytes += tm * tn * e.dtype.itemsize
    out_shape = [jax.ShapeDtypeStruct((m, n), out_dtype)]
    out_specs = [tile_spec]
    if next_gain is not None:
        out_shape += [jax.ShapeDtypeStruct((m, n), BF16), jax.ShapeDtypeStruct((m, col_tiles * LANES), F32)]
        out_specs += [tile_spec, pl.BlockSpec((tm, LANES), lambda j, i: (i, j))]
        extra_bytes += tm * tn * 2
    out_bytes = tm * tn * jnp.dtype(out_dtype).itemsize
    est = (2 * (tm * k * 2 + k * tn * w.dtype.itemsize + out_bytes + extra_bytes)
           + k * tn * 2 + 3 * tm * tn * 4)
    tile_kernel = functools.partial(_mm_kernel, mode=mode, scale=scale, n_out=len(out_shape),
                                    row_ssq=(k if row_ssq is not None else None), emit_norm=next_gain is not None)

    def stream(*hbm_refs):
        pltpu.emit_pipeline(tile_kernel, grid=(col_tiles, m // tm), in_specs=in_specs, out_specs=out_specs)(*hbm_refs)

    anywhere = pl.BlockSpec(memory_space=pl.ANY)
    outs = pl.pallas_call(
        stream,
        out_shape=out_shape,
        in_specs=[anywhere] * len(operands),
        out_specs=[anywhere] * len(out_shape),
        compiler_params=pltpu.CompilerParams(vmem_limit_bytes=_vmem_limit(est + (4 << 20))),
        name="mm_" + mode,
    )(*operands)
    return outs[0] if next_gain is None else outs


def _attn_kernel(q_ref, k_ref, v_ref, g_ref, o_ref, acc_ref, carry_ref, *, tile, heads):
    s_len = q_ref.shape[0]
    nt = s_len // tile
    row = lax.broadcasted_iota(jnp.int32, (tile, tile), 0)
    col = lax.broadcasted_iota(jnp.int32, (tile, tile), 1)
    causal = col < row
    neg_later = jnp.where(row > col, -1.0, 0.0).astype(BF16)
    for j in reversed(range(nt)):
        first = j == nt - 1
        lo, hi = j * tile, (j + 1) * tile
        for h in range(heads):
            cols = slice(h * HEAD_DIM, (h + 1) * HEAD_DIM)
            k = k_ref[lo:hi, cols]
            v = v_ref[lo:hi, cols]
            z = lax.dot_general(q_ref[lo:, cols], k, (((1,), (1,)), ((), ())), preferred_element_type=F32)
            sp = jnp.maximum(z, 0.0) + jnp.log2(1.0 + jnp.exp2(-jnp.abs(z)))
            sp_diag = jnp.where(causal, sp[:tile], 0.0)
            sp = sp_diag if first else jnp.concatenate([sp_diag, sp[tile:]], axis=0)
            suffix = jnp.dot(sp.astype(BF16), neg_later, preferred_element_type=F32)
            arg = (z - sp) + suffix
            if not first:
                carry = carry_ref[hi:, cols]
                carry = jnp.concatenate([carry] * (tile // LANES), axis=1)
                arg = jnp.concatenate([arg[:tile], arg[tile:] + carry], axis=0)
            w = jnp.exp2(arg)
            w_diag = jnp.where(causal, w[:tile], 0.0)
            w = w_diag if first else jnp.concatenate([w_diag, w[tile:]], axis=0)
            pv = jnp.dot(w.astype(BF16), v, preferred_element_type=F32)
            total = -jnp.broadcast_to(jnp.sum(sp, axis=1, keepdims=True), (sp.shape[0], LANES))
            acc_ref[lo:hi, cols] = pv[:tile]
            carry_ref[lo:hi, cols] = total[:tile]
            if not first:
                acc_ref[hi:, cols] += pv[tile:]
                carry_ref[hi:, cols] += total[tile:]
    o_ref[...] = (acc_ref[...] * g_ref[...].astype(F32)).astype(o_ref.dtype)


def _attention(q, k, v, gate, *, batch, seq):
    m, d = q.shape
    heads = 2 if d % (2 * HEAD_DIM) == 0 else 1
    width = heads * HEAD_DIM
    tile = min(256, seq)
    r3 = lambda a: a.reshape(batch, seq, d)
    spec = pl.BlockSpec((None, seq, width), lambda b, h: (b, 0, h))
    est = 10 * seq * width * 2 + 2 * seq * width * 4 + 8 * heads * seq * tile * 4
    out = pl.pallas_call(
        functools.partial(_attn_kernel, tile=tile, heads=heads),
        out_shape=jax.ShapeDtypeStruct((batch, seq, d), BF16),
        grid=(batch, d // width),
        in_specs=[spec, spec, spec, spec],
        out_specs=spec,
        scratch_shapes=[pltpu.VMEM((seq, width), F32),
                        pltpu.VMEM((seq, width), F32)],
        compiler_params=pltpu.CompilerParams(
            dimension_semantics=("parallel", "parallel"),
            vmem_limit_bytes=_vmem_limit(est + (8 << 20))),
        name="stickbreak_attn",
    )(r3(q), r3(k), r3(v), r3(gate))
    return out.reshape(m, d)


def _dot_nt_3pass(a, b):
    dims = (((1,), (1,)), ((), ()))
    a_hi = a.astype(BF16)
    b_hi = b.astype(BF16)
    a_lo = (a - a_hi.astype(F32)).astype(BF16)
    b_lo = (b - b_hi.astype(F32)).astype(BF16)
    return (lax.dot_general(a_hi, b_hi, dims, preferred_element_type=F32)
            + lax.dot_general(a_hi, b_lo, dims, preferred_element_type=F32)
            + lax.dot_general(a_lo, b_hi, dims, preferred_element_type=F32))


def _ssm_prep_kernel(p_ref, bre_ref, bim_ref, cre_ref, cim_ref, t_ref, w_ref, vt_ref, lam_ref):
    n_rows = 24
    n = lax.broadcasted_iota(jnp.int32, (n_rows, LANES), 0).astype(F32)
    lane = lax.broadcasted_iota(jnp.int32, (GROUP, LANES), 1)
    own_half = (lane < STATE, lane >= STATE)

    def per_pair(pi, _):
        a_re = p_ref[pi, 0:1, :]
        a_im = p_ref[pi, 1:2, :]
        dt = jnp.exp(p_ref[pi, 2:3, :])
        mag = jnp.exp(n * (a_re * dt))
        ang = n * (a_im * dt)
        pre = mag * jnp.cos(ang)
        pim = mag * jnp.sin(ang)
        lam_re = pre[1:2, :]
        lam_im = pim[1:2, :]
        den = a_re * a_re + a_im * a_im
        f_re = ((lam_re - 1.0) * a_re + lam_im * a_im) / den
        f_im = (lam_im * a_re - (lam_re - 1.0) * a_im) / den
        fre = f_re * pre - f_im * pim
        fim = f_re * pim + f_im * pre
        bre = bre_ref[pi]
        bim = bim_ref[pi]
        cre = cre_ref[pi]
        cim = cim_ref[pi]

        v_pair = []
        for e in range(CHUNK + 1):
            v_pair.append((cre * pre[e:e + 1, :] - cim * pim[e:e + 1, :],
                           -(cre * pim[e:e + 1, :] + cim * pre[e:e + 1, :])))
        w_pair = []
        for e in range(CHUNK):
            w_pair.append((fre[e:e + 1, :] * bre - fim[e:e + 1, :] * bim,
                           fre[e:e + 1, :] * bim + fim[e:e + 1, :] * bre))
        lam_pair = jnp.concatenate([pre[CHUNK:CHUNK + 1, :], pim[CHUNK:CHUNK + 1, :],
                                    jnp.zeros((6, LANES), F32)], axis=0)

        for parity in range(2):
            gi = 2 * pi + parity
            keep = own_half[parity]
            own = lambda ab: jnp.concatenate([jnp.where(keep, ab[0], 0.0), jnp.where(keep, ab[1], 0.0)], axis=1)
            v_blocks = [own(v) for v in v_pair]
            for t in range(CHUNK):
                vt_ref[gi, t * GROUP:(t + 1) * GROUP, :] = v_blocks[t + 1].astype(vt_ref.dtype)
            v0t = jnp.concatenate(v_blocks[:CHUNK], axis=0)
            for s in range(CHUNK):
                w_ref[gi, s * GROUP:(s + 1) * GROUP, :] = own(w_pair[CHUNK - 1 - s]).astype(w_ref.dtype)
            wb0 = own(w_pair[0])
            kmat = _dot_nt_3pass(wb0, v0t)
            k0 = kmat[:, :LANES]
            k1 = kmat[:, LANES:]
            zero = jnp.zeros_like(k0)
            for s in range(CHUNK):
                sh = s * GROUP
                if s == 0:
                    lo, hi = k0, k1
                elif sh < LANES:
                    r0 = pltpu.roll(k0, sh, 1)
                    r1 = pltpu.roll(k1, sh, 1)
                    lo = jnp.where(lane >= sh, r0, 0.0)
                    hi = jnp.where(lane >= sh, r1, r0)
                elif sh == LANES:
                    lo, hi = zero, k0
                else:
                    r0 = pltpu.roll(k0, sh - LANES, 1)
                    lo = zero
                    hi = jnp.where(lane >= sh - LANES, r0, 0.0)
                t_ref[gi, s * GROUP:(s + 1) * GROUP, :] = jnp.concatenate([lo, hi], axis=1).astype(t_ref.dtype)
            lam_ref[gi] = jnp.where(keep[:8], lam_pair, 0.0)
        return 0

    lax.fori_loop(0, p_ref.shape[0], per_pair, 0, unroll=True)


def _ssm_prep(a_re, a_im, log_dt, b_re, b_im, c_re, c_im):
    g = a_re.shape[0]

    def pair_lanes(a):
        r = a.shape[1]
        return a.reshape(g // 2, 2, r, STATE).transpose(0, 2, 1, 3).reshape(g // 2, r, LANES)

    rows = jnp.stack([a_re, a_im, jnp.broadcast_to(log_dt[:, None], (g, STATE))]
                     + [jnp.zeros((g, STATE), F32)] * 5, axis=1).astype(F32)
    params = pair_lanes(rows)
    bt_re = pair_lanes(jnp.swapaxes(b_re, 1, 2))
    bt_im = pair_lanes(jnp.swapaxes(b_im, 1, 2))
    cp_re = pair_lanes(c_re)
    cp_im = pair_lanes(c_im)
    gb = GROUPS_PER_TILE
    kk = CHUNK * GROUP
    spec8p = pl.BlockSpec((gb // 2, 8, LANES), lambda i: (i, 0, 0))
    spec16p = pl.BlockSpec((gb // 2, GROUP, LANES), lambda i: (i, 0, 0))
    spec8 = pl.BlockSpec((gb, 8, LANES), lambda i: (i, 0, 0))
    specm = pl.BlockSpec((gb, kk, kk), lambda i: (i, 0, 0))
    return pl.pallas_call(
        _ssm_prep_kernel,
        out_shape=(jax.ShapeDtypeStruct((g, kk, kk), BF16),
                   jax.ShapeDtypeStruct((g, kk, 2 * LANES), BF16),
                   jax.ShapeDtypeStruct((g, kk, 2 * LANES), BF16),
                   jax.ShapeDtypeStruct((g, 8, LANES), F32)),
        grid=(g // gb,),
        in_specs=[spec8p, spec16p, spec16p, spec16p, spec16p],
        out_specs=(specm, specm, specm, spec8),
        compiler_params=pltpu.CompilerParams(dimension_semantics=("parallel",)),
        name="ssm_prep",
    )(params, bt_re, bt_im, cp_re, cp_im)


def _ssm_kernel(u_ref, t_ref, w_ref, vt_ref, lam_ref, d_ref, o_ref, zs, ys, ss, hs, yscr, *, batch, nchunk,
                block_rows):
    n_rows = batch * nchunk
    pairs = GROUPS_PER_TILE // 2
    pitch = nchunk + SCAN_PITCH_PAD
    lane_group = lax.broadcasted_iota(jnp.int32, (block_rows, LANES), 1) // GROUP
    bit_set = {d: (lane_group & d) != 0 for d in (4, 2, 1)}

    def transpose_pieces(xs, rotate=pltpu.roll):
        for d in (4, 2, 1):
            new = list(xs)
            for m in range(GROUPS_PER_TILE):
                if m & d:
                    continue
                a, b = xs[m], xs[m + d]
                new[m] = jnp.where(bit_set[d], rotate(b, GROUP * d, 1), a)
                new[m + d] = jnp.where(bit_set[d], b, rotate(a, LANES - GROUP * d, 1))
            xs = new
        return xs

    def position_rows(r0, t):
        return pl.ds(r0 * CHUNK + t, block_rows, stride=CHUNK)

    row_starts = range(0, n_rows, block_rows)

    for r0 in row_starts:
        for half in range(2):
            pieces = transpose_pieces([u_ref[position_rows(r0, half * 8 + m), :].astype(zs.dtype)
                                       for m in range(GROUPS_PER_TILE)], jnp.roll)
            for g in range(GROUPS_PER_TILE):
                zs[g, r0:r0 + block_rows, half * LANES:(half + 1) * LANES] = pieces[g]

    for p in range(pairs):
        s_end = (jnp.dot(zs[2 * p], w_ref[2 * p], preferred_element_type=F32)
                 + jnp.dot(zs[2 * p + 1], w_ref[2 * p + 1], preferred_element_type=F32))
        for b in range(batch):
            ss[2 * p, b * pitch:b * pitch + nchunk, :] = s_end[b * nchunk:(b + 1) * nchunk, :LANES]
            ss[2 * p + 1, b * pitch:b * pitch + nchunk, :] = s_end[b * nchunk:(b + 1) * nchunk, LANES:]

    lre = [lam_ref[2 * p, 0:1, :] + lam_ref[2 * p + 1, 0:1, :] for p in range(pairs)]
    lim = [lam_ref[2 * p, 1:2, :] + lam_ref[2 * p + 1, 1:2, :] for p in range(pairs)]

    def scan(c, carry):
        new = []
        for p in range(pairs):
            h_re, h_im = carry[2 * p], carry[2 * p + 1]
            rows = pl.ds(c, batch, stride=pitch)
            hs[2 * p, rows, :] = h_re
            hs[2 * p + 1, rows, :] = h_im
            s_re = ss[2 * p, rows, :]
            s_im = ss[2 * p + 1, rows, :]
            new.append(h_re * lre[p] - h_im * lim[p] + s_re)
            new.append(h_im * lre[p] + h_re * lim[p] + s_im)
        return tuple(new)

    zero_state = jnp.zeros((batch, LANES), F32)
    lax.fori_loop(0, nchunk, scan, tuple(zero_state for _ in range(2 * pairs)))

    def entering(q):
        return jnp.concatenate([hs[q, b * pitch:b * pitch + nchunk, :] for b in range(batch)], axis=0)

    for g in range(GROUPS_PER_TILE):
        h_in = jnp.concatenate([entering(2 * (g // 2)), entering(2 * (g // 2) + 1)], axis=1).astype(BF16)
        ys[g] = (jnp.dot(zs[g], t_ref[g], preferred_element_type=F32)
                 + lax.dot_general(h_in, vt_ref[g], (((1,), (1,)), ((), ())), preferred_element_type=F32))

    d_row = d_ref[...]
    for r0 in row_starts:
        for half in range(2):
            pieces = transpose_pieces([ys[g, r0:r0 + block_rows, half * LANES:(half + 1) * LANES]
                                       for g in range(GROUPS_PER_TILE)])
            for m in range(GROUPS_PER_TILE):
                rows = position_rows(r0, half * 8 + m)
                yscr[rows, :] = jax.nn.gelu(pieces[m] + d_row * u_ref[rows, :])

    o_ref[...] = yscr[...].astype(o_ref.dtype)


def _ssm(u, tmat, wmat, vtmat, lam, d_vec, *, batch, seq):
    m, d = u.shape
    nchunk = seq // CHUNK
    n_rows = batch * nchunk
    n_pad = batch * (nchunk + SCAN_PITCH_PAD)
    gb = GROUPS_PER_TILE
    kk = CHUNK * GROUP
    block_rows = min(n_rows, 128)
    specm = pl.BlockSpec((gb, kk, kk), lambda i: (i, 0, 0))
    est = (2 * m * LANES * (4 + 2) + 2 * 3 * gb * kk * kk * 2 + gb * n_rows * kk * (2 + 4)
           + 2 * gb * n_pad * LANES * 4 + m * LANES * 4 + 8 * n_rows * kk * 4)
    return pl.pallas_call(
        functools.partial(_ssm_kernel, batch=batch, nchunk=nchunk, block_rows=block_rows),
        out_shape=jax.ShapeDtypeStruct((m, d), BF16),
        grid=(d // LANES,),
        in_specs=[pl.BlockSpec((m, LANES), lambda i: (0, i)),
                  specm, specm, specm,
                  pl.BlockSpec((gb, 8, LANES), lambda i: (i, 0, 0)),
                  pl.BlockSpec((1, LANES), lambda i: (0, i))],
        out_specs=pl.BlockSpec((m, LANES), lambda i: (0, i)),
        scratch_shapes=[pltpu.VMEM((gb, n_rows, kk), BF16),
                        pltpu.VMEM((gb, n_rows, kk), F32),
                        pltpu.VMEM((gb, n_pad, LANES), F32),
                        pltpu.VMEM((gb, n_pad, LANES), F32),
                        pltpu.VMEM((m, LANES), F32)],
        compiler_params=pltpu.CompilerParams(
            dimension_semantics=("parallel",),
            vmem_limit_bytes=_vmem_limit(est)),
        name="s5_scan",
    )(u, tmat, wmat, vtmat, lam, d_vec.reshape(1, d))


def kernel(x, norm_g, attn_w_in, attn_q_g, attn_k_g, attn_w_out, ssm_w_in, ssm_A_re, ssm_A_im, ssm_log_dt,
           ssm_B_re, ssm_B_im, ssm_C_re, ssm_C_im, ssm_D, ssm_glu_w, ssm_glu_b, ssm_w_out):
    batch, seq, d = x.shape
    m = batch * seq
    assert d % LANES == 0 and seq % 256 == 0 and (m // CHUNK) % 8 == 0
    assert ssm_A_re.shape[1:] == (d // GROUP, STATE) and (d // GROUP) % GROUPS_PER_TILE == 0
    x2 = x.reshape(m, d)

    h = _rmsnorm(x2, norm_g[0])
    w_in = attn_w_in[0]
    qg = attn_q_g[0].reshape(1, HEAD_DIM)
    kg = attn_k_g[0].reshape(1, HEAD_DIM)
    q = _matmul(h, w_in, col_off=0, n=d, mode="headnorm", out_dtype=BF16, extras=(qg,),
                scale=LOG2E / math.sqrt(HEAD_DIM))
    k = _matmul(h, w_in, col_off=d, n=d, mode="headnorm", out_dtype=BF16, extras=(kg,))
    v = _matmul(h, w_in, col_off=2 * d, n=d, mode="store", out_dtype=BF16)
    gate = _matmul(h, w_in, col_off=3 * d, n=d, mode="silu", out_dtype=BF16)
    og = _attention(q, k, v, gate, batch=batch, seq=seq)
    x2, h, ssq = _matmul(og, attn_w_out[0], col_off=0, n=d, mode="residual", out_dtype=F32, extras=(x2,),
                         next_gain=norm_g[1])

    w_in = ssm_w_in[0]
    u = _matmul(h, w_in, col_off=0, n=d, mode="store", out_dtype=F32, row_ssq=ssq)
    gate = _matmul(h, w_in, col_off=d, n=d, mode="silu", out_dtype=BF16, row_ssq=ssq)
    tmat, wmat, vtmat, lam = _ssm_prep(ssm_A_re[0], ssm_A_im[0], ssm_log_dt[0], ssm_B_re[0], ssm_B_im[0],
                                       ssm_C_re[0], ssm_C_im[0])
    y = _ssm(u, tmat, wmat, vtmat, lam, ssm_D[0], batch=batch, seq=seq)
    yg = _matmul(y, ssm_glu_w[0], col_off=0, n=d, mode="glu", out_dtype=BF16,
                 extras=(ssm_glu_b[0].reshape(1, d), y, gate))
    x2 = _matmul(yg, ssm_w_out[0], col_off=0, n=d, mode="residual", out_dtype=F32, extras=(x2,))
    return x2.reshape(batch, seq, d)
```
